```python
import functools
import jax, jax.numpy as jnp
from jax import lax
import numpy as np

D_MODEL = 1024
BATCH = 16
SEQ = 2048
DEPTH = 2
DEC_BATCH = 32
DEC_SEQ = 4
PAST_LEN = 16384
PAGE_SIZE = 128

N_MEM = 256
H_M = 4
D_M = D_MODEL // H_M
H_A = 8
D_A = 64
W_A = H_A * D_A
SB_BIAS_INIT = -8.0
H_B = 4
DK_B = 64
DV_B = 64
W_BK = H_B * DK_B
W_B = H_B * DV_B
H_C = 4
D_C = 64
W_C = H_C * D_C
LORA_W = 32
LORA_A = 32
LORA_G = 64
C_PROJ = 3 * W_C + LORA_W + LORA_A + LORA_G
D_IN = 3 * W_A + 2 * W_BK + 2 * W_B + C_PROJ
IN_SPLITS = (W_A, 2 * W_A, 3 * W_A, 3 * W_A + W_BK, 3 * W_A + 2 * W_BK,
             3 * W_A + 2 * W_BK + W_B, 3 * W_A + 2 * W_BK + 2 * W_B)
C_SPLITS = (W_C, 2 * W_C, 3 * W_C, 3 * W_C + LORA_W, 3 * W_C + LORA_W + LORA_A)
D_MIX = W_A + W_B + W_C
D_FF = 256 * ((8 * D_MODEL + 3 * 256 - 1) // (3 * 256))
Q_BLOCK = 128
HGRN_CHUNK = 64
RMS_EPS = 1e-6
GN_EPS = 64e-5
L2_EPS = 1e-12

kernel_name = 'hybrid_sb_hgrn2_rwkv7_decoder_step'


def rmsnorm(x, g):
    x32 = x.astype(jnp.float32)
    y = x32 * lax.rsqrt(jnp.mean(x32 * x32, axis=-1, keepdims=True) + RMS_EPS)
    return y.astype(x.dtype) * g


def sb_attend(q, k, v, bias, q_start):
    z = jnp.einsum('bqhd,bkhd->bhqk', q, k).astype(jnp.float32) * (D_A ** -0.5) \
        + bias.astype(jnp.float32)[None, :, None, None]
    q_pos = q_start + jnp.arange(q.shape[1])
    k_pos = jnp.arange(k.shape[1])
    mask = k_pos[None, :] < q_pos[:, None]
    log_not = jnp.where(mask, jax.nn.log_sigmoid(-z), 0.0)
    between = lax.cumsum(log_not, axis=3, reverse=True) - log_not
    w = jnp.where(mask, jnp.exp(jax.nn.log_sigmoid(z) + between), 0.0)
    return jnp.einsum('bhqk,bkhd->bqhd', w.astype(v.dtype), v)


def sb_prompt(q, k, v, bias):
    outs = []
    for blk in range(q.shape[1] // Q_BLOCK):
        s, e = blk * Q_BLOCK, (blk + 1) * Q_BLOCK
        outs.append(sb_attend(q[:, s:e], k[:, :e], v[:, :e], bias, s))
    return jnp.concatenate(outs, axis=1)


def sb_sample(q, k, v, bias, k_past, v_past):
    k_all = jnp.concatenate([k_past, k.astype(k_past.dtype)], axis=1)
    v_all = jnp.concatenate([v_past, v.astype(v_past.dtype)], axis=1)
    return sb_attend(q, k_all, v_all, bias, k_past.shape[1])


def hgrn2_recurrence(q, log_f, i, s0):
    b, L = q.shape[0], q.shape[1]
    c = HGRN_CHUNK if L % HGRN_CHUNK == 0 else L
    nc = L // c
    k = -jnp.expm1(log_f)

    def chunks(t):
        return t.astype(jnp.float32).reshape(b, nc, c, H_B, t.shape[-1]).transpose(1, 0, 3, 2, 4)

    causal = jnp.tril(jnp.ones((c, c), dtype=bool))[:, :, None]

    def step(s, xs):
        qc, lfc, kc, ic = xs
        cum = jnp.cumsum(lfc, axis=2)
        inter = jnp.einsum('bhtk,bhkv->bhtv', qc * jnp.exp(cum), s)
        decay = jnp.exp(jnp.where(causal, cum[:, :, :, None, :] - cum[:, :, None, :, :], -jnp.inf))
        scores = jnp.einsum('bhtk,bhtsk,bhsk->bhts', qc, decay, kc)
        intra = jnp.einsum('bhts,bhsv->bhtv', scores, ic)
        last = cum[:, :, -1:, :]
        s_new = s * jnp.exp(last[:, :, 0, :, None]) + jnp.einsum('bhsk,bhsv->bhkv', kc * jnp.exp(last - cum), ic)
        return s_new, inter + intra

    s_fin, o = lax.scan(step, s0.astype(jnp.float32), (chunks(q), chunks(log_f), chunks(k), chunks(i)))
    return o.transpose(1, 0, 3, 2, 4).reshape(b, L, H_B, DV_B), s_fin


def rwkv7_recurrence(r, w, k, v, a, bb, s0):
    def step(s, xs):
        rt, wt, kt, vt, at, bt = xs
        sa = jnp.einsum('bhij,bhj->bhi', s, at)
        s = s * wt[:, :, None, :] + sa[..., None] * bt[:, :, None, :] + vt[..., None] * kt[:, :, None, :]
        return s, jnp.einsum('bhij,bhj->bhi', s, rt)

    def seq_first(t):
        return t.astype(jnp.float32).transpose(1, 0, 2, 3)

    s_fin, y = lax.scan(step, s0.astype(jnp.float32),
                        (seq_first(r), seq_first(w), seq_first(k), seq_first(v), seq_first(a), seq_first(bb)))
    return y.transpose(1, 0, 2, 3), s_fin


def rwkv7_branch(c, shift0, s0, p):
    b, L = c.shape[0], c.shape[1]
    shifted = jnp.concatenate([shift0[:, None, :].astype(c.dtype), c[:, :-1]], axis=1)
    xm = c + (shifted - c) * p['rwkv_mu']
    r, k, v, xw, xa, xg = jnp.split(xm, C_SPLITS, axis=-1)
    w_log = -jax.nn.softplus(-(p['rwkv_w0'] + jnp.tanh(xw) @ p['rwkv_w_up'])) - 0.5
    decay = jnp.exp(-jnp.exp(w_log.astype(jnp.float32)))
    a = jax.nn.sigmoid(p['rwkv_a0'] + xa @ p['rwkv_a_up'])
    g = jax.nn.sigmoid(xg) @ p['rwkv_g_up']

    def heads(t):
        return t.reshape(b, L, H_C, D_C)

    kk = heads(k * p['rwkv_k_k']).astype(jnp.float32)
    kk = kk / jnp.maximum(jnp.sqrt(jnp.sum(kk * kk, axis=-1, keepdims=True)), L2_EPS)
    k = k * (1.0 + (a - 1.0) * p['rwkv_k_a'])
    rh, kh, vh, ah = heads(r), heads(k), heads(v), heads(a)
    y, s_fin = rwkv7_recurrence(rh, heads(decay), kh, vh, -kk, kk * ah, s0)
    mean = jnp.mean(y, axis=-1, keepdims=True)
    var = jnp.mean(jnp.square(y - mean), axis=-1, keepdims=True)
    y = ((y - mean) * lax.rsqrt(var + GN_EPS)).reshape(b, L, W_C) * p['rwkv_ln_w'] + p['rwkv_ln_b']
    bonus = jnp.sum(rh * kh * p['rwkv_r_k'], axis=-1, keepdims=True) * vh
    out = (y + bonus.reshape(b, L, W_C)) * g
    return out.astype(c.dtype), s_fin, c[:, -1]


def mem_attend(h, mem_k, mem_v, wq, wo):
    b, L = h.shape[0], h.shape[1]
    q = (h @ wq).reshape(b, L, H_M, D_M)
    s = jnp.einsum('blhd,bmhd->bhlm', q, mem_k).astype(jnp.float32) * (D_M ** -0.5)
    pr = jax.nn.softmax(s, axis=-1)
    o = jnp.einsum('bhlm,bmhd->blhd', pr.astype(mem_v.dtype), mem_v).reshape(b, L, D_MODEL)
    return o @ wo


def trunk_layer(x, sb_fn, s_hgrn0, s_rwkv0, shift0, mem_k, mem_v, lb, p):
    b, L = x.shape[0], x.shape[1]
    h = rmsnorm(x, p['norm_mix'])
    u = h @ p['w_in']
    a_q, a_k, a_v, b_q, b_f, b_i, b_g, c = jnp.split(u, IN_SPLITS, axis=-1)
    k_a = a_k.reshape(b, L, H_A, D_A)
    v_a = a_v.reshape(b, L, H_A, D_A)
    o_a = sb_fn(a_q.reshape(b, L, H_A, D_A), k_a, v_a, p['sb_bias']).reshape(b, L, W_A)
    log_f = jnp.logaddexp(jnp.log(lb), jnp.log1p(-lb) + jax.nn.log_sigmoid(b_f.astype(jnp.float32)))
    o_b, s_hgrn = hgrn2_recurrence(jax.nn.silu(b_q).reshape(b, L, H_B, DK_B),
                                   log_f.reshape(b, L, H_B, DK_B),
                                   b_i.reshape(b, L, H_B, DV_B), s_hgrn0)
    o_b = rmsnorm(o_b, p['hgrn_norm'].reshape(H_B, DV_B)).reshape(b, L, W_B) * jax.nn.silu(b_g)
    o_c, s_rwkv, shift = rwkv7_branch(c, shift0, s_rwkv0, p)
    mixed = jnp.concatenate([o_a, o_b.astype(x.dtype), o_c], axis=-1)
    x = x + mixed @ p['w_out']
    x = x + mem_attend(rmsnorm(x, p['norm_mem']), mem_k, mem_v, p['mem_wq'], p['mem_wo'])
    h = rmsnorm(x, p['norm_ffn'])
    x = x + (jax.nn.silu(h @ p['ffn_w_gate']) * (h @ p['ffn_w_up'])) @ p['ffn_w_down']
    return x, k_a, v_a, s_hgrn, s_rwkv, shift


def setup_inputs(seed: int = 0) -> dict:
    key = jax.random.key(seed)
    ks = iter(jax.random.split(key, 64))

    def nrm(shape, scale=1.0):
        return scale * jax.random.normal(next(ks), shape, jnp.float32)

    def gain(shape):
        return 1.0 + nrm(shape, 0.02)

    n_pages = PAST_LEN // PAGE_SIZE
    n_pool = (DEC_BATCH * n_pages * 5) // 4
    page_table = jax.random.permutation(next(ks), n_pool)[: DEC_BATCH * n_pages].reshape(DEC_BATCH, n_pages).astype(jnp.int32)
    return {
        'x_prompt': nrm((BATCH, SEQ, D_MODEL)),
        'x_sample': nrm((DEC_BATCH, DEC_SEQ, D_MODEL)),
        'mem_prompt': nrm((BATCH, N_MEM, D_MODEL)),
        'cache_sb_k': nrm((DEPTH, n_pool, PAGE_SIZE, H_A, D_A)),
        'cache_sb_v': nrm((DEPTH, n_pool, PAGE_SIZE, H_A, D_A)),
        'page_table': page_table,
        'state_hgrn': nrm((DEPTH, DEC_BATCH, H_B, DK_B, DV_B), 0.5),
        'state_rwkv': nrm((DEPTH, DEC_BATCH, H_C, D_C, D_C), 0.3),
        'state_rwkv_shift': nrm((DEPTH, DEC_BATCH, C_PROJ)),
        'cache_mem_k': nrm((DEPTH, DEC_BATCH, N_MEM, H_M, D_M)),
        'cache_mem_v': nrm((DEPTH, DEC_BATCH, N_MEM, H_M, D_M)),
        'norm_mix': gain((DEPTH, D_MODEL)),
        'w_in': nrm((DEPTH, D_MODEL, D_IN), D_MODEL ** -0.5),
        'sb_bias': SB_BIAS_INIT + nrm((DEPTH, H_A), 0.5),
        'hgrn_lb': nrm((DEPTH, W_BK), 1.0),
        'hgrn_norm': gain((DEPTH, W_B)),
        'rwkv_mu': jax.random.uniform(next(ks), (DEPTH, C_PROJ), jnp.float32),
        'rwkv_w0': nrm((DEPTH, W_C), 0.5),
        'rwkv_w_up': nrm((DEPTH, LORA_W, W_C), 0.5 * LORA_W ** -0.5),
        'rwkv_a0': nrm((DEPTH, W_C), 0.1),
        'rwkv_a_up': nrm((DEPTH, LORA_A, W_C), LORA_A ** -0.5),
        'rwkv_g_up': nrm((DEPTH, LORA_G, W_C), LORA_G ** -0.5),
        'rwkv_k_k': 0.85 + nrm((DEPTH, W_C), 0.05),
        'rwkv_k_a': 1.0 + nrm((DEPTH, W_C), 0.05),
        'rwkv_r_k': nrm((DEPTH, H_C, D_C), 0.1),
        'rwkv_ln_w': gain((DEPTH, W_C)),
        'rwkv_ln_b': nrm((DEPTH, W_C), 0.02),
        'w_out': nrm((DEPTH, D_MIX, D_MODEL), D_MIX ** -0.5),
        'norm_mem': gain((DEPTH, D_MODEL)),
        'mem_wq': nrm((DEPTH, D_MODEL, D_MODEL), D_MODEL ** -0.5),
        'mem_wk': nrm((DEPTH, D_MODEL, D_MODEL), D_MODEL ** -0.5),
        'mem_wv': nrm((DEPTH, D_MODEL, D_MODEL), D_MODEL ** -0.5),
        'mem_wo': nrm((DEPTH, D_MODEL, D_MODEL), D_MODEL ** -0.5),
        'norm_ffn': gain((DEPTH, D_MODEL)),
        'ffn_w_gate': nrm((DEPTH, D_MODEL, D_FF), D_MODEL ** -0.5),
        'ffn_w_up': nrm((DEPTH, D_MODEL, D_FF), D_MODEL ** -0.5),
        'ffn_w_down': nrm((DEPTH, D_FF, D_MODEL), D_FF ** -0.5),
        'final_norm': gain((D_MODEL,)),
    }


def reference(x_prompt, x_sample, mem_prompt, cache_sb_k, cache_sb_v, page_table, state_hgrn, state_rwkv,
              state_rwkv_shift, cache_mem_k, cache_mem_v, norm_mix, w_in, sb_bias, hgrn_lb, hgrn_norm, rwkv_mu,
              rwkv_w0, rwkv_w_up, rwkv_a0, rwkv_a_up, rwkv_g_up, rwkv_k_k, rwkv_k_a, rwkv_r_k, rwkv_ln_w, rwkv_ln_b,
              w_out, norm_mem, mem_wq, mem_wk, mem_wv, mem_wo, norm_ffn, ffn_w_gate, ffn_w_up, ffn_w_down,
              final_norm):
    bp = x_prompt.shape[0]
    bs = x_sample.shape[0]
    n_past = page_table.shape[1] * PAGE_SIZE
    lb_all = jnp.cumsum(jax.nn.softmax(hgrn_lb.astype(jnp.float32), axis=0), axis=0)
    lb_all = lb_all - lb_all[0]
    xp, xs = x_prompt, x_sample
    kp_l, vp_l, ks_l, vs_l = [], [], [], []
    hp_l, hs_l, rp_l, rs_l, shp_l, shs_l, mkp_l, mvp_l = [], [], [], [], [], [], [], []
    for l in range(DEPTH):
        p = dict(norm_mix=norm_mix[l], w_in=w_in[l], sb_bias=sb_bias[l], hgrn_norm=hgrn_norm[l],
                 rwkv_mu=rwkv_mu[l], rwkv_w0=rwkv_w0[l], rwkv_w_up=rwkv_w_up[l], rwkv_a0=rwkv_a0[l],
                 rwkv_a_up=rwkv_a_up[l], rwkv_g_up=rwkv_g_up[l], rwkv_k_k=rwkv_k_k[l], rwkv_k_a=rwkv_k_a[l],
                 rwkv_r_k=rwkv_r_k[l], rwkv_ln_w=rwkv_ln_w[l], rwkv_ln_b=rwkv_ln_b[l], w_out=w_out[l],
                 norm_mem=norm_mem[l], mem_wq=mem_wq[l], mem_wo=mem_wo[l], norm_ffn=norm_ffn[l],
                 ffn_w_gate=ffn_w_gate[l], ffn_w_up=ffn_w_up[l], ffn_w_down=ffn_w_down[l])
        mk_p = (mem_prompt @ mem_wk[l]).reshape(bp, N_MEM, H_M, D_M)
        mv_p = (mem_prompt @ mem_wv[l]).reshape(bp, N_MEM, H_M, D_M)
        xp, kp, vp, shg_p, srw_p, sh_p = trunk_layer(
            xp, sb_prompt,
            jnp.zeros((bp, H_B, DK_B, DV_B), jnp.float32),
            jnp.zeros((bp, H_C, D_C, D_C), jnp.float32),
            jnp.zeros((bp, C_PROJ), x_prompt.dtype),
            mk_p, mv_p, lb_all[l], p)
        k_past = cache_sb_k[l][page_table].reshape(bs, n_past, H_A, D_A)
        v_past = cache_sb_v[l][page_table].reshape(bs, n_past, H_A, D_A)
        sb_fn = functools.partial(sb_sample, k_past=k_past, v_past=v_past)
        xs, ks_, vs_, shg_s, srw_s, sh_s = trunk_layer(
            xs, sb_fn, state_hgrn[l], state_rwkv[l], state_rwkv_shift[l],
            cache_mem_k[l], cache_mem_v[l], lb_all[l], p)
        kp_l.append(kp); vp_l.append(vp); ks_l.append(ks_); vs_l.append(vs_)
        hp_l.append(shg_p); hs_l.append(shg_s); rp_l.append(srw_p); rs_l.append(srw_s)
        shp_l.append(sh_p); shs_l.append(sh_s); mkp_l.append(mk_p); mvp_l.append(mv_p)
    y_prompt = rmsnorm(xp, final_norm)
    y_sample = rmsnorm(xs, final_norm)
    return (y_prompt, y_sample,
            jnp.stack(kp_l), jnp.stack(vp_l), jnp.stack(ks_l), jnp.stack(vs_l),
            jnp.stack(hp_l), jnp.stack(hs_l), jnp.stack(rp_l), jnp.stack(rs_l),
            jnp.stack(shp_l), jnp.stack(shs_l), jnp.stack(mkp_l), jnp.stack(mvp_l))
```

```python
import functools

import jax
import jax.numpy as jnp
from jax import lax
from jax.experimental import pallas as pl
from jax.experimental.pallas import tpu as pltpu

F32 = jnp.float32
BF16 = jnp.bfloat16

D_MODEL = 1024
H_A, D_A = 8, 64
W_A = H_A * D_A
H_R, D_R = 4, 64
W_R = H_R * D_R
H_M, D_M = 4, 256
N_MEM = 256
C_PROJ = 3 * W_R + 128
D_IN = 3 * W_A + 4 * W_R + C_PROJ
PAGE = 128
RMS_EPS = 1e-6
GN_EPS = 64e-5
L2_EPS = 1e-12

LANES = 128
VMEM_LIMIT = 56 * 1024 * 1024

NEG_BIG = -1e30


def _cparams(sem):
    return pltpu.CompilerParams(dimension_semantics=sem, vmem_limit_bytes=VMEM_LIMIT)


def _dot(a, b):
    return jnp.dot(a, b, preferred_element_type=F32)


def _dot_nt(a, b):
    return lax.dot_general(a, b, (((1,), (1,)), ((), ())), preferred_element_type=F32)


def _dot_tn(a, b):
    return lax.dot_general(a, b, (((0,), (0,)), ((), ())), preferred_element_type=F32)


def _split2(x):
    hi = x.astype(BF16)
    lo = (x - hi.astype(F32)).astype(BF16)
    return hi, lo


def _split3(x):
    hi = x.astype(BF16)
    r = x - hi.astype(F32)
    mid = r.astype(BF16)
    lo = (r - mid.astype(F32)).astype(BF16)
    return hi, mid, lo


def _rms(x, g):
    ms = jnp.mean(x * x, axis=-1, keepdims=True)
    return (x * lax.rsqrt(ms + RMS_EPS)) * g


def _const_spec(shape):
    nd = len(shape)
    return pl.BlockSpec(shape, lambda *_: (0,) * nd)


def _in_proj_kernel(x_ref, g_ref, w_ref, q_ref, k_ref, v_ref, b_ref, c_ref):
    hb = _rms(x_ref[...], g_ref[...]).astype(BF16)
    q_ref[...] = _dot(hb, w_ref[:, 0:W_A])
    k_ref[...] = _dot(hb, w_ref[:, W_A:2 * W_A])
    v_ref[...] = _dot(hb, w_ref[:, 2 * W_A:3 * W_A])
    b_ref[...] = _dot(hb, w_ref[:, 3 * W_A:3 * W_A + 4 * W_R])
    c_ref[...] = _dot(hb, w_ref[:, 3 * W_A + 4 * W_R:D_IN])


def _in_proj(x, g, w_bf):
    m = x.shape[0]
    tm = min(m, 512)
    widths = (W_A, W_A, W_A, 4 * W_R, C_PROJ)
    return pl.pallas_call(
        _in_proj_kernel,
        grid=(m // tm,),
        in_specs=[pl.BlockSpec((tm, D_MODEL), lambda i: (i, 0)),
                  _const_spec((1, D_MODEL)),
                  _const_spec((D_MODEL, D_IN))],
        out_specs=[pl.BlockSpec((tm, w), lambda i: (i, 0)) for w in widths],
        out_shape=[jax.ShapeDtypeStruct((m, w), F32) for w in widths],
        compiler_params=_cparams(("parallel",)),
        name="in_proj",
    )(x, g.reshape(1, D_MODEL), w_bf)


def _matmul_kernel(x_ref, w_ref, o_ref):
    o_ref[...] = _dot(x_ref[...].astype(BF16), w_ref[...])


def _matmul(x, w_bf):
    m, k = x.shape
    n = w_bf.shape[1]
    tm = min(m, 512)
    return pl.pallas_call(
        _matmul_kernel,
        grid=(m // tm,),
        in_specs=[pl.BlockSpec((tm, k), lambda i: (i, 0)), _const_spec((k, n))],
        out_specs=pl.BlockSpec((tm, n), lambda i: (i, 0)),
        out_shape=jax.ShapeDtypeStruct((m, n), F32),
        compiler_params=_cparams(("parallel",)),
        name="matmul",
    )(x, w_bf)


def _out_q_kernel(x_ref, oa_ref, ob_ref, oc_ref, wo_ref, g_ref, wq_ref, x1_ref, qm_ref):
    x1 = (x_ref[...]
          + _dot(oa_ref[...].astype(BF16), wo_ref[0:W_A, :])
          + _dot(ob_ref[...].astype(BF16), wo_ref[W_A:W_A + W_R, :])
          + _dot(oc_ref[...].astype(BF16), wo_ref[W_A + W_R:W_A + 2 * W_R, :]))
    x1_ref[...] = x1
    qm_ref[...] = _dot(_rms(x1, g_ref[...]).astype(BF16), wq_ref[...])


def _out_q(x, oa, ob, oc, wo_bf, g, wq_bf):
    m = x.shape[0]
    tm = min(m, 512)
    row = lambda w: pl.BlockSpec((tm, w), lambda i: (i, 0))
    return pl.pallas_call(
        _out_q_kernel,
        grid=(m // tm,),
        in_specs=[row(D_MODEL), row(W_A), row(W_R), row(W_R),
                  _const_spec((D_MODEL, D_MODEL)), _const_spec((1, D_MODEL)),
                  _const_spec((D_MODEL, D_MODEL))],
        out_specs=[row(D_MODEL), row(D_MODEL)],
        out_shape=[jax.ShapeDtypeStruct((m, D_MODEL), F32)] * 2,
        compiler_params=_cparams(("parallel",)),
        name="out_q",
    )(x, oa, ob, oc, wo_bf, g.reshape(1, D_MODEL), wq_bf)


def _mem_attn_kernel(x_ref, q_ref, mk_ref, mv_ref, wo_ref, o_ref):
    acc = x_ref[0]
    for h in range(H_M):
        sl = slice(h * D_M, (h + 1) * D_M)
        qh = (q_ref[0, :, sl] * (D_M ** -0.5)).astype(BF16)
        s = _dot_nt(qh, mk_ref[0, :, sl].astype(BF16))
        e = jnp.exp(s - jnp.max(s, axis=-1, keepdims=True))
        p = e / jnp.sum(e, axis=-1, keepdims=True)
        oh = _dot(p.astype(BF16), mv_ref[0, :, sl].astype(BF16))
        acc = acc + _dot(oh.astype(BF16), wo_ref[sl, :])
    o_ref[0] = acc


def _mem_attn(x1, qm, mk, mv, wo_bf):
    b, l, _ = x1.shape
    tl = min(l, 512)
    tok = pl.BlockSpec((1, tl, D_MODEL), lambda i, j: (i, j, 0))
    mem = pl.BlockSpec((1, N_MEM, D_MODEL), lambda i, j: (i, 0, 0))
    return pl.pallas_call(
        _mem_attn_kernel,
        grid=(b, l // tl),
        in_specs=[tok, tok, mem, mem, _const_spec((D_MODEL, D_MODEL))],
        out_specs=tok,
        out_shape=jax.ShapeDtypeStruct((b, l, D_MODEL), F32),
        compiler_params=_cparams(("parallel", "parallel")),
        name="mem_attn",
    )(x1, qm, mk, mv, wo_bf)


FF_CHUNK = 256


def _ffn_kernel(x_ref, g_ref, wg_ref, wu_ref, wd_ref, gf_ref, o_ref, acc_ref, *, final_norm):
    x = x_ref[...]
    hb = _rms(x, g_ref[...]).astype(BF16)
    acc_ref[...] = x
    d_ff = wg_ref.shape[1]
    for c in range(d_ff // FF_CHUNK):
        sl = slice(c * FF_CHUNK, (c + 1) * FF_CHUNK)
        gate = _dot(hb, wg_ref[:, sl])
        up = _dot(hb, wu_ref[:, sl])
        a = (gate * jax.nn.sigmoid(gate)) * up
        acc_ref[...] += _dot(a.astype(BF16), wd_ref[sl, :])
    if final_norm:
        o_ref[...] = _rms(acc_ref[...], gf_ref[...])
    else:
        o_ref[...] = acc_ref[...]


def _ffn(x, g, wg_bf, wu_bf, wd_bf, gf, final_norm):
    m = x.shape[0]
    d_ff = wg_bf.shape[1]
    tm = min(m, 512)
    row = pl.BlockSpec((tm, D_MODEL), lambda i: (i, 0))
    return pl.pallas_call(
        functools.partial(_ffn_kernel, final_norm=final_norm),
        grid=(m // tm,),
        in_specs=[row, _const_spec((1, D_MODEL)), _const_spec((D_MODEL, d_ff)),
                  _const_spec((D_MODEL, d_ff)), _const_spec((d_ff, D_MODEL)),
                  _const_spec((1, D_MODEL))],
        out_specs=row,
        out_shape=jax.ShapeDtypeStruct((m, D_MODEL), F32),
        scratch_shapes=[pltpu.VMEM((tm, D_MODEL), F32)],
        compiler_params=_cparams(("parallel",)),
        name="ffn",
    )(x, g.reshape(1, D_MODEL), wg_bf, wu_bf, wd_bf, gf.reshape(1, D_MODEL))


def _log_terms(z):
    l1p = jnp.log1p(jnp.exp(-jnp.abs(z)))
    return -(jnp.maximum(z, 0.0) + l1p), jnp.minimum(z, 0.0) - l1p


def _sb_prompt_kernel(bias_ref, q_ref, k_ref, v_ref, u_ref, o_ref, qs_ref, kb_ref, vs_ref, carry_ref):
    i = pl.program_id(1)
    nkb = k_ref.shape[1] // PAGE
    lane = lax.broadcasted_iota(jnp.int32, (PAGE, LANES), 1)
    row = lax.broadcasted_iota(jnp.int32, (PAGE, LANES), 0)
    even = lane < D_A
    causal = lane < row

    @pl.when(i == 0)
    def _():
        kb_ref[...] = k_ref[0].astype(BF16)

        def fill(j, _):
            rows = pl.ds(pl.multiple_of(j * PAGE, PAGE), PAGE)
            for p in range(H_A // 2):
                v2 = v_ref[0, rows, p * LANES:(p + 1) * LANES]
                vs_ref[p, j, 0:PAGE, :] = jnp.where(even, v2, 0.0).astype(BF16)
                vs_ref[p, j, PAGE:2 * PAGE, :] = jnp.where(even, 0.0, v2).astype(BF16)
            return 0

        lax.fori_loop(0, nkb, fill, 0)

    for p in range(H_A // 2):
        q2 = q_ref[0, :, p * LANES:(p + 1) * LANES] * (D_A ** -0.5)
        qs_ref[p, 0:PAGE, :] = jnp.where(even, q2, 0.0).astype(BF16)
        qs_ref[p, PAGE:2 * PAGE, :] = jnp.where(even, 0.0, q2).astype(BF16)

    def block(j, diag):
        rows = pl.ds(pl.multiple_of(j * PAGE, PAGE), PAGE)
        for p in range(H_A // 2):
            s2 = _dot_nt(qs_ref[p], kb_ref[rows, p * LANES:(p + 1) * LANES])
            ws = []
            for e in range(2):
                h = 2 * p + e
                z = s2[e * PAGE:(e + 1) * PAGE] + bias_ref[h]
                ln, ls = _log_terms(z)
                if diag:
                    ln = jnp.where(causal, ln, 0.0)
                hi, lo = _split2(ln)
                r = _dot(hi, u_ref[...]) + _dot(lo, u_ref[...])
                between, tot = r[:, 0:LANES], r[:, LANES:2 * LANES]
                if diag:
                    w = jnp.where(causal, jnp.exp(ls + between), 0.0)
                    carry_ref[h] = tot
                else:
                    c = carry_ref[h]
                    w = jnp.exp(ls + between + c)
                    carry_ref[h] = c + tot
                ws.append(w.astype(BF16))
            pv = _dot(jnp.concatenate(ws, axis=1), vs_ref[p, j])
            if diag:
                o_ref[0, :, p * LANES:(p + 1) * LANES] = pv
            else:
                o_ref[0, :, p * LANES:(p + 1) * LANES] += pv

    block(i, True)

    def older(t, _):
        block(i - 1 - t, False)
        return 0

    lax.fori_loop(0, i, older, 0)


def _sb_tri():
    j = lax.broadcasted_iota(jnp.int32, (PAGE, 2 * LANES), 0)
    s = lax.broadcasted_iota(jnp.int32, (PAGE, 2 * LANES), 1)
    return ((j > s) | (s >= LANES)).astype(BF16)


def _sb_prompt(q, k, v, bias):
    b, l, _ = q.shape
    nq = l // PAGE
    tok = pl.BlockSpec((1, PAGE, W_A), lambda i, j: (i, j, 0))
    seq = pl.BlockSpec((1, l, W_A), lambda i, j: (i, 0, 0))
    return pl.pallas_call(
        _sb_prompt_kernel,
        grid=(b, nq),
        in_specs=[pl.BlockSpec(memory_space=pltpu.SMEM), tok, seq, seq,
                  _const_spec((PAGE, 2 * LANES))],
        out_specs=tok,
        out_shape=jax.ShapeDtypeStruct((b, l, W_A), F32),
        scratch_shapes=[pltpu.VMEM((H_A // 2, 2 * PAGE, LANES), BF16),
                        pltpu.VMEM((l, W_A), BF16),
                        pltpu.VMEM((H_A // 2, nq, 2 * PAGE, LANES), BF16),
                        pltpu.VMEM((H_A, PAGE, LANES), F32)],
        compiler_params=_cparams(("parallel", "arbitrary")),
        name="sb_prompt",
    )(bias, q, k, v, _sb_tri())


N_DEC = 4


def _sb_sample_kernel(pt_ref, qbd_ref, bl_ref, kn_ref, vn_ref, kp_ref, vp_ref, u_ref, o_ref,
                      acc_ref, carry_ref):
    del pt_ref
    j = pl.program_id(1)
    key = lax.broadcasted_iota(jnp.int32, (PAGE, LANES), 0)
    col = lax.broadcasted_iota(jnp.int32, (PAGE, LANES), 1)
    new_mask = key < col // H_A

    def process(kblk, vblk, new):
        z = _dot(kblk.astype(BF16), qbd_ref[0]) + bl_ref[...]
        ln, ls = _log_terms(z)
        if new:
            ln = jnp.where(new_mask, ln, 0.0)
        hi, lo = _split2(ln)
        between = _dot(u_ref[...], hi) + _dot(u_ref[...], lo)
        tot = jnp.sum(ln, axis=0, keepdims=True)
        if new:
            w = jnp.where(new_mask, jnp.exp(ls + between), 0.0)
            carry_ref[...] = tot
            acc_ref[...] = _dot(w.T.astype(BF16), vblk.astype(BF16))
        else:
            c = carry_ref[...]
            w = jnp.exp(ls + between + c)
            carry_ref[...] = c + tot
            acc_ref[...] += _dot(w.T.astype(BF16), vblk.astype(BF16))

    @pl.when(j == 0)
    def _():
        process(kn_ref[0], vn_ref[0], True)

    @pl.when(j > 0)
    def _():
        process(kp_ref[0, 0], vp_ref[0, 0], False)

    @pl.when(j == pl.num_programs(1) - 1)
    def _():
        head = lax.broadcasted_iota(jnp.int32, (H_A, W_A), 0)
        lane_head = lax.broadcasted_iota(jnp.int32, (H_A, W_A), 1) // D_A
        for t in range(N_DEC):
            rows = acc_ref[t * H_A:(t + 1) * H_A, :]
            o_ref[0, t:t + 1, :] = jnp.sum(jnp.where(head == lane_head, rows, 0.0), axis=0, keepdims=True)


def _sb_sample(q, k, v, bias, cache_k, cache_v, layer, page_table):
    b = q.shape[0]
    n_pages = page_table.shape[1]
    eye = jnp.eye(H_A, dtype=F32)
    q4 = q.reshape(b, N_DEC, H_A, D_A) * (D_A ** -0.5)
    qbd = jnp.einsum('bthd,hg->bhdtg', q4, eye).reshape(b, W_A, N_DEC * H_A)
    qbd = jnp.pad(qbd, ((0, 0), (0, 0), (0, LANES - N_DEC * H_A))).astype(BF16)
    bias_lane = jnp.pad(jnp.tile(bias, N_DEC), (0, LANES - N_DEC * H_A)).reshape(1, LANES)
    pad_new = lambda a: jnp.pad(a, ((0, 0), (0, PAGE - N_DEC), (0, 0)))
    tri = (lax.broadcasted_iota(jnp.int32, (PAGE, PAGE), 1)
           > lax.broadcasted_iota(jnp.int32, (PAGE, PAGE), 0)).astype(BF16)

    def page_map(i, j, pt):
        return (layer, pt[i, n_pages - jnp.maximum(j, 1)], 0, 0)

    per_b = lambda shape: pl.BlockSpec((1,) + shape, lambda i, j, pt: (i, 0, 0))
    grid_spec = pltpu.PrefetchScalarGridSpec(
        num_scalar_prefetch=1,
        grid=(b, n_pages + 1),
        in_specs=[per_b((W_A, LANES)),
                  pl.BlockSpec((1, LANES), lambda i, j, pt: (0, 0)),
                  per_b((PAGE, W_A)), per_b((PAGE, W_A)),
                  pl.BlockSpec((1, 1, PAGE, W_A), page_map),
                  pl.BlockSpec((1, 1, PAGE, W_A), page_map),
                  pl.BlockSpec((PAGE, PAGE), lambda i, j, pt: (0, 0))],
        out_specs=per_b((N_DEC, W_A)),
        scratch_shapes=[pltpu.VMEM((LANES, W_A), F32), pltpu.VMEM((1, LANES), F32)],
    )
    return pl.pallas_call(
        _sb_sample_kernel,
        grid_spec=grid_spec,
        out_shape=jax.ShapeDtypeStruct((b, N_DEC, W_A), F32),
        compiler_params=_cparams(("parallel", "arbitrary")),
        name="sb_sample",
    )(page_table, qbd, bias_lane, pad_new(k), pad_new(v), cache_k, cache_v, tri)


def _mm(a, b, kind="nn", passes=1):
    dot = {"nn": _dot, "nt": _dot_nt, "tn": _dot_tn}[kind]
    if passes == 1:
        return dot(a.astype(BF16), b.astype(BF16))
    a_hi, a_lo = _split2(a)
    if passes == 2:
        b_hi = b.astype(BF16)
        return dot(a_hi, b_hi) + dot(a_lo, b_hi)
    b_hi, b_lo = _split2(b)
    return dot(a_hi, b_hi) + (dot(a_hi, b_lo) + dot(a_lo, b_hi))


def _head_ones(n):
    r = lax.broadcasted_iota(jnp.int32, (n, n), 0) // D_R
    c = lax.broadcasted_iota(jnp.int32, (n, n), 1) // D_R
    return (r == c).astype(BF16)


def _head_sum(x, ones):
    hi, mid, lo = _split3(x)
    return _dot(hi, ones) + (_dot(mid, ones) + _dot(lo, ones))


def _softplus(x):
    return jnp.maximum(x, 0.0) + jnp.log1p(jnp.exp(-jnp.abs(x)))


HG_SUB = 16


def _hgrn_kernel(q_ref, f_ref, i_ref, g_ref, la_ref, l1_ref, gn_ref, s0_ref, bt_ref, ones_ref,
                 o_ref, sf_ref, s_ref, *, valid):
    c = pl.program_id(2)

    @pl.when(c == 0)
    def _():
        s_ref[...] = s0_ref[0, 0]

    tc = q_ref.shape[1]
    bq = q_ref[0]
    qs = bq * jax.nn.sigmoid(bq)
    bf = f_ref[0]
    la = la_ref[...]
    l1p = jnp.log1p(jnp.exp(-jnp.abs(bf)))
    lb = l1_ref[...] + (jnp.minimum(bf, 0.0) - l1p)
    lf = jnp.maximum(la, lb) + jnp.log1p(jnp.exp(-jnp.abs(la - lb)))
    kk = jnp.exp(l1_ref[...] - (jnp.maximum(bf, 0.0) + l1p))
    if valid < tc:
        live = lax.broadcasted_iota(jnp.int32, (tc, LANES), 0) < valid
        lf = jnp.where(live, lf, 0.0)
        kk = jnp.where(live, kk, 0.0)
    hi, mid, lo = _split3(lf)
    cum = _dot(bt_ref[...], hi) + (_dot(bt_ref[...], mid) + _dot(bt_ref[...], lo))
    iv = i_ref[0]

    trow = lax.broadcasted_iota(jnp.int32, (HG_SUB, LANES), 0)
    r2 = lax.broadcasted_iota(jnp.int32, (LANES, LANES), 0) // D_R
    c2 = lax.broadcasted_iota(jnp.int32, (LANES, LANES), 1) // D_R
    same_head = r2 == c2

    s = s_ref[...]
    outs = []
    for sc in range(tc // HG_SUB):
        rows = slice(sc * HG_SUB, (sc + 1) * HG_SUB)
        cumc, qc, kc, ic = cum[rows], qs[rows], kk[rows], iv[rows]
        last = cumc[HG_SUB - 1:HG_SUB]
        inter = _dot_nt((qc * jnp.exp(cumc)).astype(BF16), s.astype(BF16))
        ps = []
        for j in range(HG_SUB):
            d = jnp.where(trow >= j, cumc - cumc[j:j + 1], NEG_BIG)
            ps.append(jnp.exp(d) * qc * kc[j:j + 1])
        scores = _dot(jnp.concatenate(ps, axis=0).astype(BF16), ones_ref[...])
        intra = scores[0:HG_SUB] * ic[0:1]
        for j in range(1, HG_SUB):
            intra = intra + scores[j * HG_SUB:(j + 1) * HG_SUB] * ic[j:j + 1]
        outs.append(inter + intra)
        upd = _dot_tn(ic.astype(BF16), (kc * jnp.exp(last - cumc)).astype(BF16))
        s = s * jnp.exp(last) + jnp.where(same_head, upd, 0.0)
    s_ref[...] = s

    o = outs[0] if len(outs) == 1 else jnp.concatenate(outs, axis=0)
    ms = _head_sum(o * o, ones_ref[...]) * (1.0 / D_R)
    bg = g_ref[0]
    o_ref[0] = (o * lax.rsqrt(ms + RMS_EPS)) * gn_ref[...] * (bg * jax.nn.sigmoid(bg))

    @pl.when(c == pl.num_programs(2) - 1)
    def _():
        sf_ref[0, 0] = s


def _hgrn(bslab, log_lb, log1m_lb, gnorm, s0bd, valid):
    b, l, _ = bslab.shape
    tc = min(l, 128)
    col = lambda k: pl.BlockSpec((1, tc, LANES), lambda i, p, c: (i, c, 2 * k + p))
    par = pl.BlockSpec((1, LANES), lambda i, p, c: (0, p))
    st = pl.BlockSpec((1, 1, LANES, LANES), lambda i, p, c: (i, p, 0, 0))
    t = lax.broadcasted_iota(jnp.int32, (tc, tc), 0)
    s = lax.broadcasted_iota(jnp.int32, (tc, tc), 1)
    blk_tri = ((t // HG_SUB == s // HG_SUB) & (s <= t)).astype(BF16)
    return pl.pallas_call(
        functools.partial(_hgrn_kernel, valid=valid),
        grid=(b, 2, l // tc),
        in_specs=[col(0), col(1), col(2), col(3), par, par, par, st,
                  _const_spec((tc, tc)), _const_spec((LANES, LANES))],
        out_specs=[pl.BlockSpec((1, tc, LANES), lambda i, p, c: (i, c, p)), st],
        out_shape=[jax.ShapeDtypeStruct((b, l, W_R), F32),
                   jax.ShapeDtypeStruct((b, 2, LANES, LANES), F32)],
        scratch_shapes=[pltpu.VMEM((LANES, LANES), F32)],
        compiler_params=_cparams(("parallel", "parallel", "arbitrary")),
        name="hgrn",
    )(bslab, bslab, bslab, bslab, log_lb.reshape(1, W_R), log1m_lb.reshape(1, W_R),
      gnorm.reshape(1, W_R), s0bd, blk_tri, _head_ones(LANES))


def _pair_state_in(s):
    b = s.shape[0]
    st = jnp.swapaxes(s, 2, 3).reshape(b, 2, 2, D_R, D_R)
    return jnp.einsum('bpevk,ef->bpevfk', st, jnp.eye(2, dtype=s.dtype)).reshape(b, 2, LANES, LANES)


def _pair_state_out(sbd):
    b = sbd.shape[0]
    s6 = sbd.reshape(b, 2, 2, D_R, 2, D_R)
    st = jnp.stack([s6[:, :, 0, :, 0, :], s6[:, :, 1, :, 1, :]], axis=2)
    return jnp.swapaxes(st.reshape(b, H_R, D_R, D_R), 2, 3)


RW_CHUNK = 64
RW_N = H_R * RW_CHUNK


def _rwkv_kernel(c_ref, sh0_ref, s0_ref, mu_ref, w0_ref, wup_ref, a0_ref, aup_ref, gup_ref,
                 kk_ref, ka_ref, rk_ref, lnw_ref, lnb_ref, tri_ref, ones_ref,
                 o_ref, sf_ref, prev_ref, s_ref, *, valid, passes):
    c = pl.program_id(1)

    @pl.when(c == 0)
    def _():
        prev_ref[...] = sh0_ref[0]
        s_ref[...] = s0_ref[0]

    cb = c_ref[0]
    tok = lax.broadcasted_iota(jnp.int32, (RW_CHUNK, C_PROJ), 0)
    shifted = jnp.where(tok == 0, prev_ref[...], pltpu.roll(cb, 1, axis=0))
    prev_ref[...] = cb[RW_CHUNK - 1:RW_CHUNK]
    xm = cb + (shifted - cb) * mu_ref[...]
    r = xm[:, 0:W_R]
    k = xm[:, W_R:2 * W_R]
    v = xm[:, 2 * W_R:3 * W_R]
    tail = xm[:, 3 * W_R:C_PROJ]
    ones = ones_ref[...]

    w_log = -_softplus(-(w0_ref[...] + _mm(jnp.tanh(tail), wup_ref[...]))) - 0.5
    lw = -jnp.exp(w_log)
    a = jax.nn.sigmoid(a0_ref[...] + _mm(tail, aup_ref[...]))
    g = _mm(jax.nn.sigmoid(tail), gup_ref[...])
    kk = k * kk_ref[...]
    kk = kk / jnp.maximum(jnp.sqrt(_head_sum(kk * kk, ones)), L2_EPS)
    k = k * (1.0 + (a - 1.0) * ka_ref[...])
    av, bv = -kk, kk * a
    if valid < RW_CHUNK:
        live = lax.broadcasted_iota(jnp.int32, (RW_CHUNK, W_R), 0) < valid
        zero = lambda x: jnp.where(live, x, 0.0)
        lw, av, bv, k, v = zero(lw), zero(av), zero(bv), zero(k), zero(v)

    hi, mid, lo = _split3(lw)
    cl = _dot(tri_ref[...], hi) + (_dot(tri_ref[...], mid) + _dot(tri_ref[...], lo))
    last = cl[RW_CHUNK - 1:RW_CHUNK]
    e_neg = jnp.exp(-cl)
    e_end = jnp.exp(last - cl)

    lane_head = lax.broadcasted_iota(jnp.int32, (RW_CHUNK, W_R), 1) // D_R

    def stack(x):
        return jnp.concatenate([jnp.where(lane_head == h, x, 0.0) for h in range(H_R)], axis=0)

    m_a = stack(av * jnp.exp(cl - lw))
    m_b = stack(bv * e_neg)
    m_k = stack(k * e_neg)
    m_r = stack(r * jnp.exp(cl))
    m_v = stack(v)
    m_bh = stack(bv * e_end)
    m_kh = stack(k * e_end)

    row = lax.broadcasted_iota(jnp.int32, (RW_N, RW_N), 0)
    col = lax.broadcasted_iota(jnp.int32, (RW_N, RW_N), 1)
    mm = functools.partial(_mm, passes=passes)
    t_ab = jnp.where(row > col, mm(m_a, m_b, "nt"), 0.0)
    t_ak = jnp.where(row > col, mm(m_a, m_k, "nt"), 0.0)
    t_rb = jnp.where(row >= col, mm(m_r, m_b, "nt"), 0.0)
    t_rk = jnp.where(row >= col, mm(m_r, m_k, "nt"), 0.0)

    inv = jnp.where(row == col, 1.0, t_ab)
    pw = t_ab
    for _ in range(5):
        pw = mm(pw, pw)
        inv = inv + mm(inv, pw)

    w1 = mm(inv, m_a)
    w2 = mm(inv, mm(t_ak, m_v))
    rm = m_r + mm(t_rb, w1)
    yc = mm(t_rb, w2) + mm(t_rk, m_v)
    pm = jnp.where(row == col, jnp.exp(last), 0.0) + mm(w1, m_bh, "tn")
    qm = mm(w2, m_bh, "tn") + mm(m_v, m_kh, "tn")

    s = s_ref[...]
    y_st = mm(rm, s, "nt") + yc
    s_ref[...] = mm(s, pm) + qm
    y = (y_st[0:RW_CHUNK] + y_st[RW_CHUNK:2 * RW_CHUNK]) + (y_st[2 * RW_CHUNK:3 * RW_CHUNK] + y_st[3 * RW_CHUNK:])

    mean = _head_sum(y, ones) * (1.0 / D_R)
    yc0 = y - mean
    var = _head_sum(yc0 * yc0, ones) * (1.0 / D_R)
    yn = yc0 * lax.rsqrt(var + GN_EPS) * lnw_ref[...] + lnb_ref[...]
    bonus = _head_sum(r * k * rk_ref[...], ones) * v
    o_ref[0] = (yn + bonus) * g

    @pl.when(c == pl.num_programs(1) - 1)
    def _():
        sf_ref[0] = s_ref[...]


def _rwkv(cslab, shift0, s0bd, p, valid, passes=3):
    b, l, _ = cslab.shape
    row = lambda a: a.reshape(1, -1)
    pad_up = lambda w, off: jnp.zeros((LANES, W_R), F32).at[off:off + w.shape[0]].set(w).astype(BF16)
    t = lax.broadcasted_iota(jnp.int32, (RW_CHUNK, RW_CHUNK), 0)
    s = lax.broadcasted_iota(jnp.int32, (RW_CHUNK, RW_CHUNK), 1)
    consts = [row(p['rwkv_mu']), row(p['rwkv_w0']), pad_up(p['rwkv_w_up'], 0), row(p['rwkv_a0']),
              pad_up(p['rwkv_a_up'], 32), pad_up(p['rwkv_g_up'], 64), row(p['rwkv_k_k']),
              row(p['rwkv_k_a']), row(p['rwkv_r_k']), row(p['rwkv_ln_w']), row(p['rwkv_ln_b']),
              (s <= t).astype(BF16), _head_ones(W_R)]
    st = pl.BlockSpec((1, RW_N, RW_N), lambda i, c: (i, 0, 0))
    return pl.pallas_call(
        functools.partial(_rwkv_kernel, valid=valid, passes=passes),
        grid=(b, l // RW_CHUNK),
        in_specs=[pl.BlockSpec((1, RW_CHUNK, C_PROJ), lambda i, c: (i, c, 0)),
                  pl.BlockSpec((1, 1, C_PROJ), lambda i, c: (i, 0, 0)), st]
                 + [_const_spec(a.shape) for a in consts],
        out_specs=[pl.BlockSpec((1, RW_CHUNK, W_R), lambda i, c: (i, c, 0)), st],
        out_shape=[jax.ShapeDtypeStruct((b, l, W_R), F32),
                   jax.ShapeDtypeStruct((b, RW_N, RW_N), F32)],
        scratch_shapes=[pltpu.VMEM((1, C_PROJ), F32), pltpu.VMEM((RW_N, RW_N), F32)],
        compiler_params=_cparams(("parallel", "arbitrary")),
        name="rwkv",
    )(cslab, shift0, s0bd, *consts)


def _quad_state_in(s):
    b = s.shape[0]
    return jnp.einsum('bhij,hg->bhigj', s, jnp.eye(H_R, dtype=s.dtype)).reshape(b, RW_N, RW_N)


def _quad_state_out(sbd):
    b = sbd.shape[0]
    s5 = sbd.reshape(b, H_R, D_R, H_R, D_R)
    return jnp.stack([s5[:, h, :, h, :] for h in range(H_R)], axis=1)


def _pad_tokens(a, multiple):
    pad = (-a.shape[1]) % multiple
    return a if pad == 0 else jnp.pad(a, ((0, 0), (0, pad), (0, 0)))


def _trunk_layer(x, b, l, sb_fn, s_hgrn0, s_rwkv0, shift0, mem_k, mem_v, lb, p, final_gain):
    q, k, v, bslab, cslab = _in_proj(x, p['norm_mix'], p['w_in'])
    o_a = sb_fn(q.reshape(b, l, W_A), k.reshape(b, l, W_A), v.reshape(b, l, W_A))
    o_b, s_hgrn = _hgrn(_pad_tokens(bslab.reshape(b, l, 4 * W_R), HG_SUB), jnp.log(lb), jnp.log1p(-lb),
                        p['hgrn_norm'], _pair_state_in(s_hgrn0), l)
    cslab = cslab.reshape(b, l, C_PROJ)
    o_c, s_rwkv = _rwkv(_pad_tokens(cslab, RW_CHUNK), shift0.reshape(b, 1, C_PROJ),
                        _quad_state_in(s_rwkv0), p, l)
    x1, qm = _out_q(x, o_a.reshape(b * l, W_A), o_b[:, :l].reshape(b * l, W_R),
                    o_c[:, :l].reshape(b * l, W_R), p['w_out'], p['norm_mem'], p['mem_wq'])
    x2 = _mem_attn(x1.reshape(b, l, D_MODEL), qm.reshape(b, l, D_MODEL), mem_k, mem_v, p['mem_wo'])
    x3 = _ffn(x2.reshape(b * l, D_MODEL), p['norm_ffn'], p['ffn_w_gate'], p['ffn_w_up'], p['ffn_w_down'],
              p['norm_ffn'] if final_gain is None else final_gain, final_gain is not None)
    return (x3, k.reshape(b, l, H_A, D_A), v.reshape(b, l, H_A, D_A), _pair_state_out(s_hgrn),
            _quad_state_out(s_rwkv), cslab[:, l - 1])


def kernel(x_prompt, x_sample, mem_prompt, cache_sb_k, cache_sb_v, page_table, state_hgrn, state_rwkv,
           state_rwkv_shift, cache_mem_k, cache_mem_v, norm_mix, w_in, sb_bias, hgrn_lb, hgrn_norm, rwkv_mu,
           rwkv_w0, rwkv_w_up, rwkv_a0, rwkv_a_up, rwkv_g_up, rwkv_k_k, rwkv_k_a, rwkv_r_k, rwkv_ln_w, rwkv_ln_b,
           w_out, norm_mem, mem_wq, mem_wk, mem_wv, mem_wo, norm_ffn, ffn_w_gate, ffn_w_up, ffn_w_down,
           final_norm):
    bp, lp, _ = x_prompt.shape
    bs, ls, _ = x_sample.shape
    depth = w_in.shape[0]
    n_pool = cache_sb_k.shape[1]
    lb_all = jnp.cumsum(jax.nn.softmax(hgrn_lb.astype(F32), axis=0), axis=0)
    lb_all = lb_all - lb_all[0]
    cache_k = cache_sb_k.reshape(depth, n_pool, PAGE, W_A)
    cache_v = cache_sb_v.reshape(depth, n_pool, PAGE, W_A)
    mem_tok = mem_prompt.reshape(bp * N_MEM, D_MODEL)
    xp = x_prompt.reshape(bp * lp, D_MODEL)
    xs = x_sample.reshape(bs * ls, D_MODEL)
    outs = [[] for _ in range(12)]
    for l in range(depth):
        bf = lambda w: w[l].astype(BF16)
        p = dict(norm_mix=norm_mix[l], w_in=bf(w_in), hgrn_norm=hgrn_norm[l], rwkv_mu=rwkv_mu[l],
                 rwkv_w0=rwkv_w0[l], rwkv_w_up=rwkv_w_up[l], rwkv_a0=rwkv_a0[l], rwkv_a_up=rwkv_a_up[l],
                 rwkv_g_up=rwkv_g_up[l], rwkv_k_k=rwkv_k_k[l], rwkv_k_a=rwkv_k_a[l], rwkv_r_k=rwkv_r_k[l],
                 rwkv_ln_w=rwkv_ln_w[l], rwkv_ln_b=rwkv_ln_b[l], w_out=bf(w_out), norm_mem=norm_mem[l],
                 mem_wq=bf(mem_wq), mem_wo=bf(mem_wo), norm_ffn=norm_ffn[l], ffn_w_gate=bf(ffn_w_gate),
                 ffn_w_up=bf(ffn_w_up), ffn_w_down=bf(ffn_w_down))
        gain = final_norm if l == depth - 1 else None
        bias = sb_bias[l]
        mk_p = _matmul(mem_tok, bf(mem_wk)).reshape(bp, N_MEM, D_MODEL)
        mv_p = _matmul(mem_tok, bf(mem_wv)).reshape(bp, N_MEM, D_MODEL)
        xp, kp, vp, hg_p, rw_p, sh_p = _trunk_layer(
            xp, bp, lp, functools.partial(_sb_prompt, bias=bias),
            jnp.zeros((bp, H_R, D_R, D_R), F32), jnp.zeros((bp, H_R, D_R, D_R), F32),
            jnp.zeros((bp, C_PROJ), F32), mk_p, mv_p, lb_all[l], p, gain)
        sb_fn = functools.partial(_sb_sample, bias=bias, cache_k=cache_k, cache_v=cache_v, layer=l,
                                  page_table=page_table)
        xs, ks, vs, hg_s, rw_s, sh_s = _trunk_layer(
            xs, bs, ls, sb_fn, state_hgrn[l], state_rwkv[l], state_rwkv_shift[l],
            cache_mem_k[l].reshape(bs, N_MEM, D_MODEL), cache_mem_v[l].reshape(bs, N_MEM, D_MODEL),
            lb_all[l], p, gain)
        for lst, val in zip(outs, (kp, vp, ks, vs, hg_p, hg_s, rw_p, rw_s, sh_p, sh_s,
                                   mk_p.reshape(bp, N_MEM, H_M, D_M), mv_p.reshape(bp, N_MEM, H_M, D_M))):
            lst.append(val)
    return (xp.reshape(bp, lp, D_MODEL), xs.reshape(bs, ls, D_MODEL)) + tuple(jnp.stack(o) for o in outs)
```

```python
import functools

import jax
import jax.numpy as jnp
from jax import lax
from jax.experimental import pallas as pl
from jax.experimental.pallas import tpu as pltpu

F32 = jnp.float32
BF16 = jnp.bfloat16

D_MODEL = 1024
H_A, D_A = 8, 64
W_A = H_A * D_A
H_R, D_R = 4, 64
W_R = H_R * D_R
H_M, D_M = 4, 256
N_MEM = 256
C_PROJ = 3 * W_R + 128
D_IN = 3 * W_A + 4 * W_R + C_PROJ
PAGE = 128
RMS_EPS = 1e-6
GN_EPS = 64e-5
L2_EPS = 1e-12

LANES = 128
VMEM_LIMIT = 56 * 1024 * 1024

NEG_BIG = -1e30


def _cparams(sem):
    return pltpu.CompilerParams(dimension_semantics=sem, vmem_limit_bytes=VMEM_LIMIT)


def _dot(a, b):
    return jnp.dot(a, b, preferred_element_type=F32)


def _dot_nt(a, b):
    return lax.dot_general(a, b, (((1,), (1,)), ((), ())), preferred_element_type=F32)


def _dot_tn(a, b):
    return lax.dot_general(a, b, (((0,), (0,)), ((), ())), preferred_element_type=F32)


def _split2(x):
    hi = x.astype(BF16)
    lo = (x - hi.astype(F32)).astype(BF16)
    return hi, lo


def _split3(x):
    hi = x.astype(BF16)
    r = x - hi.astype(F32)
    mid = r.astype(BF16)
    lo = (r - mid.astype(F32)).astype(BF16)
    return hi, mid, lo


def _rms(x, g):
    ms = jnp.mean(x * x, axis=-1, keepdims=True)
    return (x * lax.rsqrt(ms + RMS_EPS)) * g


def _log1p_exp(x):
    return jnp.log(1.0 + jnp.exp(-jnp.abs(x)))


def _const_spec(shape):
    nd = len(shape)
    return pl.BlockSpec(shape, lambda *_: (0,) * nd)


def _in_proj_kernel(x_ref, g_ref, w_ref, q_ref, k_ref, v_ref, b_ref, c_ref):
    hb = _rms(x_ref[...], g_ref[...]).astype(BF16)
    q_ref[...] = _dot(hb, w_ref[:, 0:W_A])
    k_ref[...] = _dot(hb, w_ref[:, W_A:2 * W_A])
    v_ref[...] = _dot(hb, w_ref[:, 2 * W_A:3 * W_A])
    b_ref[...] = _dot(hb, w_ref[:, 3 * W_A:3 * W_A + 4 * W_R])
    c_ref[...] = _dot(hb, w_ref[:, 3 * W_A + 4 * W_R:D_IN])


def _in_proj(x, g, w_bf):
    m = x.shape[0]
    tm = min(m, 512)
    widths = (W_A, W_A, W_A, 4 * W_R, C_PROJ)
    return pl.pallas_call(
        _in_proj_kernel,
        grid=(m // tm,),
        in_specs=[pl.BlockSpec((tm, D_MODEL), lambda i: (i, 0)),
                  _const_spec((1, D_MODEL)),
                  _const_spec((D_MODEL, D_IN))],
        out_specs=[pl.BlockSpec((tm, w), lambda i: (i, 0)) for w in widths],
        out_shape=[jax.ShapeDtypeStruct((m, w), F32) for w in widths],
        compiler_params=_cparams(("parallel",)),
        name="in_proj",
    )(x, g.reshape(1, D_MODEL), w_bf)


def _in_proj_t_kernel(x_ref, g_ref, w_ref, wkvt_ref, q_ref, kt_ref, vt_ref, b_ref, c_ref):
    hb = _rms(x_ref[...], g_ref[...]).astype(BF16)
    q_ref[...] = _dot(hb, w_ref[:, 0:W_A])
    kt_ref[0] = _dot_nt(wkvt_ref[0:W_A, :], hb)
    vt_ref[0] = _dot_nt(wkvt_ref[W_A:2 * W_A, :], hb)
    b_ref[...] = _dot(hb, w_ref[:, 3 * W_A:3 * W_A + 4 * W_R])
    c_ref[...] = _dot(hb, w_ref[:, 3 * W_A + 4 * W_R:D_IN])


def _in_proj_t(x, g, w_bf, b, l):
    m = x.shape[0]
    tm = min(l, 512)
    per_seq = l // tm
    wkvt = w_bf[:, W_A:3 * W_A].T
    row = lambda w: pl.BlockSpec((tm, w), lambda i: (i, 0))
    tr = pl.BlockSpec((1, W_A, tm), lambda i: (i // per_seq, 0, i % per_seq))
    return pl.pallas_call(
        _in_proj_t_kernel,
        grid=(m // tm,),
        in_specs=[row(D_MODEL), _const_spec((1, D_MODEL)), _const_spec((D_MODEL, D_IN)),
                  _const_spec((2 * W_A, D_MODEL))],
        out_specs=[row(W_A), tr, tr, row(4 * W_R), row(C_PROJ)],
        out_shape=[jax.ShapeDtypeStruct((m, W_A), F32),
                   jax.ShapeDtypeStruct((b, W_A, l), F32),
                   jax.ShapeDtypeStruct((b, W_A, l), F32),
                   jax.ShapeDtypeStruct((m, 4 * W_R), F32),
                   jax.ShapeDtypeStruct((m, C_PROJ), F32)],
        compiler_params=_cparams(("parallel",)),
        name="in_proj_t",
    )(x, g.reshape(1, D_MODEL), w_bf, wkvt)


def _matmul_kernel(x_ref, w_ref, o_ref):
    o_ref[...] = _dot(x_ref[...].astype(BF16), w_ref[...])


def _matmul(x, w_bf):
    m, k = x.shape
    n = w_bf.shape[1]
    tm = min(m, 512)
    return pl.pallas_call(
        _matmul_kernel,
        grid=(m // tm,),
        in_specs=[pl.BlockSpec((tm, k), lambda i: (i, 0)), _const_spec((k, n))],
        out_specs=pl.BlockSpec((tm, n), lambda i: (i, 0)),
        out_shape=jax.ShapeDtypeStruct((m, n), F32),
        compiler_params=_cparams(("parallel",)),
        name="matmul",
    )(x, w_bf)


def _out_q_kernel(x_ref, oa_ref, ob_ref, oc_ref, wo_ref, g_ref, wq_ref, x1_ref, qm_ref):
    x1 = (x_ref[...]
          + _dot(oa_ref[...].astype(BF16), wo_ref[0:W_A, :])
          + _dot(ob_ref[...].astype(BF16), wo_ref[W_A:W_A + W_R, :])
          + _dot(oc_ref[...].astype(BF16), wo_ref[W_A + W_R:W_A + 2 * W_R, :]))
    x1_ref[...] = x1
    qm_ref[...] = _dot(_rms(x1, g_ref[...]).astype(BF16), wq_ref[...])


def _out_q(x, oa, ob, oc, wo_bf, g, wq_bf):
    m = x.shape[0]
    tm = min(m, 512)
    row = lambda w: pl.BlockSpec((tm, w), lambda i: (i, 0))
    return pl.pallas_call(
        _out_q_kernel,
        grid=(m // tm,),
        in_specs=[row(D_MODEL), row(W_A), row(W_R), row(W_R),
                  _const_spec((D_MODEL, D_MODEL)), _const_spec((1, D_MODEL)),
                  _const_spec((D_MODEL, D_MODEL))],
        out_specs=[row(D_MODEL), row(D_MODEL)],
        out_shape=[jax.ShapeDtypeStruct((m, D_MODEL), F32)] * 2,
        compiler_params=_cparams(("parallel",)),
        name="out_q",
    )(x, oa, ob, oc, wo_bf, g.reshape(1, D_MODEL), wq_bf)


def _mem_attn_kernel(x_ref, q_ref, mk_ref, mv_ref, wo_ref, o_ref):
    acc = x_ref[0]
    for h in range(H_M):
        sl = slice(h * D_M, (h + 1) * D_M)
        qh = (q_ref[0, :, sl] * (D_M ** -0.5)).astype(BF16)
        s = _dot_nt(qh, mk_ref[0, :, sl].astype(BF16))
        e = jnp.exp(s - jnp.max(s, axis=-1, keepdims=True))
        p = e / jnp.sum(e, axis=-1, keepdims=True)
        oh = _dot(p.astype(BF16), mv_ref[0, :, sl].astype(BF16))
        acc = acc + _dot(oh.astype(BF16), wo_ref[sl, :])
    o_ref[0] = acc


def _mem_attn(x1, qm, mk, mv, wo_bf):
    b, l, _ = x1.shape
    tl = min(l, 512)
    tok = pl.BlockSpec((1, tl, D_MODEL), lambda i, j: (i, j, 0))
    mem = pl.BlockSpec((1, N_MEM, D_MODEL), lambda i, j: (i, 0, 0))
    return pl.pallas_call(
        _mem_attn_kernel,
        grid=(b, l // tl),
        in_specs=[tok, tok, mem, mem, _const_spec((D_MODEL, D_MODEL))],
        out_specs=tok,
        out_shape=jax.ShapeDtypeStruct((b, l, D_MODEL), F32),
        compiler_params=_cparams(("parallel", "parallel")),
        name="mem_attn",
    )(x1, qm, mk, mv, wo_bf)


FF_CHUNK = 256


def _ffn_kernel(x_ref, g_ref, wg_ref, wu_ref, wd_ref, gf_ref, o_ref, acc_ref, *, final_norm):
    x = x_ref[...]
    hb = _rms(x, g_ref[...]).astype(BF16)
    acc_ref[...] = x
    d_ff = wg_ref.shape[1]
    for c in range(d_ff // FF_CHUNK):
        sl = slice(c * FF_CHUNK, (c + 1) * FF_CHUNK)
        gate = _dot(hb, wg_ref[:, sl])
        up = _dot(hb, wu_ref[:, sl])
        a = (gate * jax.nn.sigmoid(gate)) * up
        acc_ref[...] += _dot(a.astype(BF16), wd_ref[sl, :])
    if final_norm:
        o_ref[...] = _rms(acc_ref[...], gf_ref[...])
    else:
        o_ref[...] = acc_ref[...]


def _ffn(x, g, wg_bf, wu_bf, wd_bf, gf, final_norm):
    m = x.shape[0]
    d_ff = wg_bf.shape[1]
    tm = min(m, 512)
    row = pl.BlockSpec((tm, D_MODEL), lambda i: (i, 0))
    return pl.pallas_call(
        functools.partial(_ffn_kernel, final_norm=final_norm),
        grid=(m // tm,),
        in_specs=[row, _const_spec((1, D_MODEL)), _const_spec((D_MODEL, d_ff)),
                  _const_spec((D_MODEL, d_ff)), _const_spec((d_ff, D_MODEL)),
                  _const_spec((1, D_MODEL))],
        out_specs=row,
        out_shape=jax.ShapeDtypeStruct((m, D_MODEL), F32),
        scratch_shapes=[pltpu.VMEM((tm, D_MODEL), F32)],
        compiler_params=_cparams(("parallel",)),
        name="ffn",
    )(x, g.reshape(1, D_MODEL), wg_bf, wu_bf, wd_bf, gf.reshape(1, D_MODEL))


def _log_terms(z):
    l1p = _log1p_exp(z)
    return -(jnp.maximum(z, 0.0) + l1p), jnp.minimum(z, 0.0) - l1p


def _sb_prompt_kernel(bias_ref, q_ref, kt_ref, vt_ref, u_ref, o_ref, qs_ref, kb_ref, vs_ref, carry_ref,
                      ls_ref, hl_ref, w_ref):
    i = pl.program_id(1)
    nkb = kt_ref.shape[2] // PAGE
    lane = lax.broadcasted_iota(jnp.int32, (PAGE, LANES), 1)
    row = lax.broadcasted_iota(jnp.int32, (PAGE, LANES), 0)
    even = lane < D_A
    even_row = row < D_A
    causal = lane < row

    @pl.when(i == 0)
    def _():
        for j in range(nkb):
            cols = slice(j * PAGE, (j + 1) * PAGE)
            kb_ref[j] = kt_ref[0, :, cols].astype(BF16)
            for p in range(H_A // 2):
                v2 = vt_ref[0, p * LANES:(p + 1) * LANES, cols]
                vs_ref[p, j, :, 0:PAGE] = jnp.where(even_row, v2, 0.0).astype(BF16)
                vs_ref[p, j, :, PAGE:2 * PAGE] = jnp.where(even_row, 0.0, v2).astype(BF16)

    for p in range(H_A // 2):
        q2 = q_ref[0, :, p * LANES:(p + 1) * LANES] * (D_A ** -0.5)
        qs_ref[p, 0:PAGE, :] = jnp.where(even, q2, 0.0).astype(BF16)
        qs_ref[p, PAGE:2 * PAGE, :] = jnp.where(even, 0.0, q2).astype(BF16)

    def block(j, diag):
        for p in range(H_A // 2):
            s2 = _dot(qs_ref[p], kb_ref[j, p * LANES:(p + 1) * LANES, :])
            for e in range(2):
                h = 2 * p + e
                z = s2[e * PAGE:(e + 1) * PAGE] + bias_ref[h]
                ln, ls = _log_terms(z)
                if diag:
                    ln = jnp.where(causal, ln, 0.0)
                hi, lo = _split2(ln)
                ls_ref[h] = ls
                hl_ref[h, :, 0:PAGE] = hi
                hl_ref[h, :, PAGE:2 * PAGE] = lo
        for h in range(H_A):
            r = _dot(hl_ref[h], u_ref[...])
            between, tot = r[:, 0:LANES], r[:, LANES:2 * LANES]
            if diag:
                w = jnp.where(causal, jnp.exp(ls_ref[h] + between), 0.0)
                carry_ref[h] = tot
            else:
                c = carry_ref[h]
                w = jnp.exp(ls_ref[h] + between + c)
                carry_ref[h] = c + tot
            w_ref[h // 2, :, (h % 2) * PAGE:(h % 2 + 1) * PAGE] = w.astype(BF16)
        for p in range(H_A // 2):
            pv = _dot_nt(w_ref[p], vs_ref[p, j])
            if diag:
                o_ref[0, :, p * LANES:(p + 1) * LANES] = pv
            else:
                o_ref[0, :, p * LANES:(p + 1) * LANES] += pv

    block(i, True)

    def older(t, _):
        block(i - 1 - t, False)
        return 0

    lax.fori_loop(0, i, older, 0)


def _sb_tri():
    j = lax.broadcasted_iota(jnp.int32, (2 * PAGE, 2 * LANES), 0) % PAGE
    s = lax.broadcasted_iota(jnp.int32, (2 * PAGE, 2 * LANES), 1)
    return ((j > s) | (s >= LANES)).astype(BF16)


def _sb_prompt(q, kt, vt, bias):
    b, l, _ = q.shape
    nq = l // PAGE
    tok = pl.BlockSpec((1, PAGE, W_A), lambda i, j: (i, j, 0))
    seq = pl.BlockSpec((1, W_A, l), lambda i, j: (i, 0, 0))
    return pl.pallas_call(
        _sb_prompt_kernel,
        grid=(b, nq),
        in_specs=[pl.BlockSpec(memory_space=pltpu.SMEM), tok, seq, seq,
                  _const_spec((2 * PAGE, 2 * LANES))],
        out_specs=tok,
        out_shape=jax.ShapeDtypeStruct((b, l, W_A), F32),
        scratch_shapes=[pltpu.VMEM((H_A // 2, 2 * PAGE, LANES), BF16),
                        pltpu.VMEM((nq, W_A, PAGE), BF16),
                        pltpu.VMEM((H_A // 2, nq, LANES, 2 * PAGE), BF16),
                        pltpu.VMEM((H_A, PAGE, LANES), F32),
                        pltpu.VMEM((H_A, PAGE, LANES), F32),
                        pltpu.VMEM((H_A, PAGE, 2 * PAGE), BF16),
                        pltpu.VMEM((H_A // 2, PAGE, 2 * PAGE), BF16)],
        compiler_params=_cparams(("parallel", "arbitrary")),
        name="sb_prompt",
    )(bias, q, kt, vt, _sb_tri())


N_DEC = 4


SB_PAGES = 8
N_QROW = N_DEC * H_A


def _sb_sample_kernel(pt_ref, q_ref, br_ref, kn_ref, vn_ref, *rest):
    del pt_ref
    kp_refs, vp_refs = rest[:SB_PAGES], rest[SB_PAGES:2 * SB_PAGES]
    u_ref, o_ref, acc_ref, carry_ref = rest[2 * SB_PAGES:]
    j = pl.program_id(1)

    def logits(kt):
        z = _dot(q_ref[0], kt.astype(BF16)) + br_ref[...]
        return _log_terms(z)

    def suffix(ln):
        hi, lo = _split2(ln)
        r = _dot(jnp.concatenate([hi, lo], axis=1), u_ref[...])
        return r[:, 0:LANES], r[:, LANES:2 * LANES]

    @pl.when(j == 0)
    def _():
        key = lax.broadcasted_iota(jnp.int32, (N_QROW, LANES), 1)
        tok = lax.broadcasted_iota(jnp.int32, (N_QROW, LANES), 0) // H_A
        visible = key < tok
        ln, ls = logits(kn_ref[0])
        ln = jnp.where(visible, ln, 0.0)
        between, tot = suffix(ln)
        w = jnp.where(visible, jnp.exp(ls + between), 0.0)
        carry_ref[...] = tot
        acc_ref[...] = _dot_nt(w.astype(BF16), vn_ref[0].astype(BF16))

    @pl.when(j > 0)
    def _():
        parts = []
        for i in range(SB_PAGES):
            ln, ls = logits(kp_refs[i][0, 0].reshape(W_A, PAGE))
            between, tot = suffix(ln)
            parts.append((ls + between, tot))
        c = carry_ref[...]
        acc = acc_ref[...]
        for i in range(SB_PAGES):
            e0, tot = parts[i]
            w = jnp.exp(e0 + c)
            c = c + tot
            acc = acc + _dot_nt(w.astype(BF16), vp_refs[i][0, 0].reshape(W_A, PAGE).astype(BF16))
        carry_ref[...] = c
        acc_ref[...] = acc

    @pl.when(j == pl.num_programs(1) - 1)
    def _():
        head = lax.broadcasted_iota(jnp.int32, (H_A, W_A), 0)
        lane_head = lax.broadcasted_iota(jnp.int32, (H_A, W_A), 1) // D_A
        for t in range(N_DEC):
            rows = acc_ref[t * H_A:(t + 1) * H_A, :]
            o_ref[0, t:t + 1, :] = jnp.sum(jnp.where(head == lane_head, rows, 0.0), axis=0, keepdims=True)


def _sb_sample(q, k, v, bias, cache_kt, cache_vt, layer, page_table):
    b = q.shape[0]
    n_pages = page_table.shape[1]
    eye = jnp.eye(H_A, dtype=F32)
    q4 = q.reshape(b, N_DEC, H_A, D_A) * (D_A ** -0.5)
    qbd = jnp.einsum('bthd,hg->btghd', q4, eye).reshape(b, N_QROW, W_A).astype(BF16)
    bias_rows = jnp.broadcast_to(jnp.tile(bias, N_DEC)[:, None], (N_QROW, LANES))
    new_t = lambda a: jnp.pad(jnp.swapaxes(a, 1, 2), ((0, 0), (0, 0), (0, PAGE - N_DEC)))

    def page_map(slot):
        def index(i, j, pt):
            return (layer, pt[i, n_pages - 1 - ((jnp.maximum(j, 1) - 1) * SB_PAGES + slot)], 0, 0, 0)
        return pl.BlockSpec((1, 1, H_A, D_A, PAGE), index)

    per_b = lambda shape: pl.BlockSpec((1,) + shape, lambda i, j, pt: (i, 0, 0))
    pages = [page_map(s) for s in range(SB_PAGES)]
    grid_spec = pltpu.PrefetchScalarGridSpec(
        num_scalar_prefetch=1,
        grid=(b, n_pages // SB_PAGES + 1),
        in_specs=[per_b((N_QROW, W_A)),
                  pl.BlockSpec((N_QROW, LANES), lambda i, j, pt: (0, 0)),
                  per_b((W_A, PAGE)), per_b((W_A, PAGE))] + pages + pages
                 + [pl.BlockSpec((2 * PAGE, 2 * LANES), lambda i, j, pt: (0, 0))],
        out_specs=per_b((N_DEC, W_A)),
        scratch_shapes=[pltpu.VMEM((N_QROW, W_A), F32), pltpu.VMEM((N_QROW, LANES), F32)],
    )
    return pl.pallas_call(
        _sb_sample_kernel,
        grid_spec=grid_spec,
        out_shape=jax.ShapeDtypeStruct((b, N_DEC, W_A), F32),
        compiler_params=_cparams(("parallel", "arbitrary")),
        name="sb_sample",
    )(page_table, qbd, bias_rows, new_t(k), new_t(v), *([cache_kt] * SB_PAGES), *([cache_vt] * SB_PAGES),
      _sb_tri())


def _mm(a, b, kind="nn", passes=1):
    dot = {"nn": _dot, "nt": _dot_nt, "tn": _dot_tn}[kind]
    if passes == 1:
        return dot(a.astype(BF16), b.astype(BF16))
    a_hi, a_lo = _split2(a)
    if passes == 2:
        b_hi = b.astype(BF16)
        return dot(a_hi, b_hi) + dot(a_lo, b_hi)
    b_hi, b_lo = _split2(b)
    return dot(a_hi, b_hi) + (dot(a_hi, b_lo) + dot(a_lo, b_hi))


def _head_ones(n):
    r = lax.broadcasted_iota(jnp.int32, (n, n), 0) // D_R
    c = lax.broadcasted_iota(jnp.int32, (n, n), 1) // D_R
    return (r == c).astype(BF16)


def _head_sum(x, ones):
    hi, mid, lo = _split3(x)
    return _dot(hi, ones) + (_dot(mid, ones) + _dot(lo, ones))


def _softplus(x):
    return jnp.maximum(x, 0.0) + _log1p_exp(x)


HG_SUB = 16


def _hgrn_kernel(q_ref, f_ref, i_ref, g_ref, la_ref, l1_ref, gn_ref, s0_ref, bt_ref, ones_ref,
                 o_ref, sf_ref, s_ref, *, valid):
    c = pl.program_id(2)

    @pl.when(c == 0)
    def _():
        s_ref[...] = s0_ref[0, 0]

    tc = q_ref.shape[1]
    bq = q_ref[0]
    qs = bq * jax.nn.sigmoid(bq)
    bf = f_ref[0]
    la = la_ref[...]
    l1p = _log1p_exp(bf)
    lb = l1_ref[...] + (jnp.minimum(bf, 0.0) - l1p)
    lf = jnp.maximum(la, lb) + _log1p_exp(la - lb)
    kk = jnp.exp(l1_ref[...] - (jnp.maximum(bf, 0.0) + l1p))
    if valid < tc:
        live = lax.broadcasted_iota(jnp.int32, (tc, LANES), 0) < valid
        lf = jnp.where(live, lf, 0.0)
        kk = jnp.where(live, kk, 0.0)
    hi, mid, lo = _split3(lf)
    cum = _dot(bt_ref[...], hi) + (_dot(bt_ref[...], mid) + _dot(bt_ref[...], lo))
    iv = i_ref[0]

    trow = lax.broadcasted_iota(jnp.int32, (HG_SUB, LANES), 0)
    r2 = lax.broadcasted_iota(jnp.int32, (LANES, LANES), 0) // D_R
    c2 = lax.broadcasted_iota(jnp.int32, (LANES, LANES), 1) // D_R
    same_head = r2 == c2

    s = s_ref[...]
    outs = []
    for sc in range(tc // HG_SUB):
        rows = slice(sc * HG_SUB, (sc + 1) * HG_SUB)
        cumc, qc, kc, ic = cum[rows], qs[rows], kk[rows], iv[rows]
        last = cumc[HG_SUB - 1:HG_SUB]
        inter = _dot_nt((qc * jnp.exp(cumc)).astype(BF16), s.astype(BF16))
        ps = []
        for j in range(HG_SUB):
            d = jnp.where(trow >= j, cumc - cumc[j:j + 1], NEG_BIG)
            ps.append(jnp.exp(d) * qc * kc[j:j + 1])
        scores = _dot(jnp.concatenate(ps, axis=0).astype(BF16), ones_ref[...])
        intra = scores[0:HG_SUB] * ic[0:1]
        for j in range(1, HG_SUB):
            intra = intra + scores[j * HG_SUB:(j + 1) * HG_SUB] * ic[j:j + 1]
        outs.append(inter + intra)
        upd = _dot_tn(ic.astype(BF16), (kc * jnp.exp(last - cumc)).astype(BF16))
        s = s * jnp.exp(last) + jnp.where(same_head, upd, 0.0)
    s_ref[...] = s

    o = outs[0] if len(outs) == 1 else jnp.concatenate(outs, axis=0)
    ms = _head_sum(o * o, ones_ref[...]) * (1.0 / D_R)
    bg = g_ref[0]
    o_ref[0] = (o * lax.rsqrt(ms + RMS_EPS)) * gn_ref[...] * (bg * jax.nn.sigmoid(bg))

    @pl.when(c == pl.num_programs(2) - 1)
    def _():
        sf_ref[0, 0] = s


def _hgrn(bslab, log_lb, log1m_lb, gnorm, s0bd, valid):
    b, l, _ = bslab.shape
    tc = min(l, 128)
    col = lambda k: pl.BlockSpec((1, tc, LANES), lambda i, p, c: (i, c, 2 * k + p))
    par = pl.BlockSpec((1, LANES), lambda i, p, c: (0, p))
    st = pl.BlockSpec((1, 1, LANES, LANES), lambda i, p, c: (i, p, 0, 0))
    t = lax.broadcasted_iota(jnp.int32, (tc, tc), 0)
    s = lax.broadcasted_iota(jnp.int32, (tc, tc), 1)
    blk_tri = ((t // HG_SUB == s // HG_SUB) & (s <= t)).astype(BF16)
    return pl.pallas_call(
        functools.partial(_hgrn_kernel, valid=valid),
        grid=(b, 2, l // tc),
        in_specs=[col(0), col(1), col(2), col(3), par, par, par, st,
                  _const_spec((tc, tc)), _const_spec((LANES, LANES))],
        out_specs=[pl.BlockSpec((1, tc, LANES), lambda i, p, c: (i, c, p)), st],
        out_shape=[jax.ShapeDtypeStruct((b, l, W_R), F32),
                   jax.ShapeDtypeStruct((b, 2, LANES, LANES), F32)],
        scratch_shapes=[pltpu.VMEM((LANES, LANES), F32)],
        compiler_params=_cparams(("parallel", "parallel", "arbitrary")),
        name="hgrn",
    )(bslab, bslab, bslab, bslab, log_lb.reshape(1, W_R), log1m_lb.reshape(1, W_R),
      gnorm.reshape(1, W_R), s0bd, blk_tri, _head_ones(LANES))


def _pair_state_in(s):
    b = s.shape[0]
    st = jnp.swapaxes(s, 2, 3).reshape(b, 2, 2, D_R, D_R)
    return jnp.einsum('bpevk,ef->bpevfk', st, jnp.eye(2, dtype=s.dtype)).reshape(b, 2, LANES, LANES)


def _pair_state_out(sbd):
    b = sbd.shape[0]
    s6 = sbd.reshape(b, 2, 2, D_R, 2, D_R)
    st = jnp.stack([s6[:, :, 0, :, 0, :], s6[:, :, 1, :, 1, :]], axis=2)
    return jnp.swapaxes(st.reshape(b, H_R, D_R, D_R), 2, 3)


RW_CHUNK = 64
RW_N = H_R * RW_CHUNK


def _rwkv_kernel(c_ref, sh0_ref, s0_ref, mu_ref, w0_ref, wup_ref, a0_ref, aup_ref, gup_ref,
                 kk_ref, ka_ref, rk_ref, lnw_ref, lnb_ref, tri_ref, ones_ref,
                 o_ref, sf_ref, prev_ref, s_ref, *, valid, passes):
    c = pl.program_id(1)

    @pl.when(c == 0)
    def _():
        prev_ref[...] = sh0_ref[0]
        s_ref[...] = s0_ref[0]

    cb = c_ref[0]
    tok = lax.broadcasted_iota(jnp.int32, (RW_CHUNK, C_PROJ), 0)
    shifted = jnp.where(tok == 0, prev_ref[...], pltpu.roll(cb, 1, axis=0))
    prev_ref[...] = cb[RW_CHUNK - 1:RW_CHUNK]
    xm = cb + (shifted - cb) * mu_ref[...]
    r = xm[:, 0:W_R]
    k = xm[:, W_R:2 * W_R]
    v = xm[:, 2 * W_R:3 * W_R]
    tail = xm[:, 3 * W_R:C_PROJ]
    ones = ones_ref[...]

    w_log = -_softplus(-(w0_ref[...] + _mm(jnp.tanh(tail), wup_ref[...]))) - 0.5
    lw = -jnp.exp(w_log)
    a = jax.nn.sigmoid(a0_ref[...] + _mm(tail, aup_ref[...]))
    g = _mm(jax.nn.sigmoid(tail), gup_ref[...])
    kk = k * kk_ref[...]
    kk = kk / jnp.maximum(jnp.sqrt(_head_sum(kk * kk, ones)), L2_EPS)
    k = k * (1.0 + (a - 1.0) * ka_ref[...])
    av, bv = -kk, kk * a
    if valid < RW_CHUNK:
        live = lax.broadcasted_iota(jnp.int32, (RW_CHUNK, W_R), 0) < valid
        zero = lambda x: jnp.where(live, x, 0.0)
        lw, av, bv, k, v = zero(lw), zero(av), zero(bv), zero(k), zero(v)

    hi, mid, lo = _split3(lw)
    cl = _dot(tri_ref[...], hi) + (_dot(tri_ref[...], mid) + _dot(tri_ref[...], lo))
    last = cl[RW_CHUNK - 1:RW_CHUNK]
    e_neg = jnp.exp(-cl)
    e_end = jnp.exp(last - cl)

    lane_head = lax.broadcasted_iota(jnp.int32, (RW_CHUNK, W_R), 1) // D_R

    def stack(x):
        return jnp.concatenate([jnp.where(lane_head == h, x, 0.0) for h in range(H_R)], axis=0)

    m_a = stack(av * jnp.exp(cl - lw))
    m_b = stack(bv * e_neg)
    m_k = stack(k * e_neg)
    m_r = stack(r * jnp.exp(cl))
    m_v = stack(v)
    m_bh = stack(bv * e_end)
    m_kh = stack(k * e_end)

    row = lax.broadcasted_iota(jnp.int32, (RW_N, RW_N), 0)
    col = lax.broadcasted_iota(jnp.int32, (RW_N, RW_N), 1)
    mm = functools.partial(_mm, passes=passes)
    t_ab = jnp.where(row > col, mm(m_a, m_b, "nt"), 0.0)
    t_ak = jnp.where(row > col, mm(m_a, m_k, "nt"), 0.0)
    t_rb = jnp.where(row >= col, mm(m_r, m_b, "nt"), 0.0)
    t_rk = jnp.where(row >= col, mm(m_r, m_k, "nt"), 0.0)

    inv = jnp.where(row == col, 1.0, t_ab)
    pw = t_ab
    for _ in range(5):
        pw = mm(pw, pw)
        inv = inv + mm(inv, pw)

    w1 = mm(inv, m_a)
    w2 = mm(inv, mm(t_ak, m_v))
    rm = m_r + mm(t_rb, w1)
    yc = mm(t_rb, w2) + mm(t_rk, m_v)
    pm = jnp.where(row == col, jnp.exp(last), 0.0) + mm(w1, m_bh, "tn")
    qm = mm(w2, m_bh, "tn") + mm(m_v, m_kh, "tn")

    s = s_ref[...]
    y_st = mm(rm, s, "nt") + yc
    s_ref[...] = _mm(s, pm, passes=max(passes, 2)) + qm
    y = (y_st[0:RW_CHUNK] + y_st[RW_CHUNK:2 * RW_CHUNK]) + (y_st[2 * RW_CHUNK:3 * RW_CHUNK] + y_st[3 * RW_CHUNK:])

    mean = _head_sum(y, ones) * (1.0 / D_R)
    yc0 = y - mean
    var = _head_sum(yc0 * yc0, ones) * (1.0 / D_R)
    yn = yc0 * lax.rsqrt(var + GN_EPS) * lnw_ref[...] + lnb_ref[...]
    bonus = _head_sum(r * k * rk_ref[...], ones) * v
    o_ref[0] = (yn + bonus) * g

    @pl.when(c == pl.num_programs(1) - 1)
    def _():
        sf_ref[0] = s_ref[...]


def _rwkv(cslab, shift0, s0bd, p, valid, passes=1):
    b, l, _ = cslab.shape
    row = lambda a: a.reshape(1, -1)
    pad_up = lambda w, off: jnp.zeros((LANES, W_R), F32).at[off:off + w.shape[0]].set(w).astype(BF16)
    t = lax.broadcasted_iota(jnp.int32, (RW_CHUNK, RW_CHUNK), 0)
    s = lax.broadcasted_iota(jnp.int32, (RW_CHUNK, RW_CHUNK), 1)
    consts = [row(p['rwkv_mu']), row(p['rwkv_w0']), pad_up(p['rwkv_w_up'], 0), row(p['rwkv_a0']),
              pad_up(p['rwkv_a_up'], 32), pad_up(p['rwkv_g_up'], 64), row(p['rwkv_k_k']),
              row(p['rwkv_k_a']), row(p['rwkv_r_k']), row(p['rwkv_ln_w']), row(p['rwkv_ln_b']),
              (s <= t).astype(BF16), _head_ones(W_R)]
    st = pl.BlockSpec((1, RW_N, RW_N), lambda i, c: (i, 0, 0))
    return pl.pallas_call(
        functools.partial(_rwkv_kernel, valid=valid, passes=passes),
        grid=(b, l // RW_CHUNK),
        in_specs=[pl.BlockSpec((1, RW_CHUNK, C_PROJ), lambda i, c: (i, c, 0)),
                  pl.BlockSpec((1, 1, C_PROJ), lambda i, c: (i, 0, 0)), st]
                 + [_const_spec(a.shape) for a in consts],
        out_specs=[pl.BlockSpec((1, RW_CHUNK, W_R), lambda i, c: (i, c, 0)), st],
        out_shape=[jax.ShapeDtypeStruct((b, l, W_R), F32),
                   jax.ShapeDtypeStruct((b, RW_N, RW_N), F32)],
        scratch_shapes=[pltpu.VMEM((1, C_PROJ), F32), pltpu.VMEM((RW_N, RW_N), F32)],
        compiler_params=_cparams(("parallel", "arbitrary")),
        name="rwkv",
    )(cslab, shift0, s0bd, *consts)


def _quad_state_in(s):
    b = s.shape[0]
    return jnp.einsum('bhij,hg->bhigj', s, jnp.eye(H_R, dtype=s.dtype)).reshape(b, RW_N, RW_N)


def _quad_state_out(sbd):
    b = sbd.shape[0]
    s5 = sbd.reshape(b, H_R, D_R, H_R, D_R)
    return jnp.stack([s5[:, h, :, h, :] for h in range(H_R)], axis=1)


def _pad_tokens(a, multiple):
    pad = (-a.shape[1]) % multiple
    return a if pad == 0 else jnp.pad(a, ((0, 0), (0, pad), (0, 0)))


def _trunk_layer(x, b, l, sb_fn, s_hgrn0, s_rwkv0, shift0, mem_k, mem_v, lb, p, final_gain, prompt):
    if prompt:
        q, kt, vt, bslab, cslab = _in_proj_t(x, p['norm_mix'], p['w_in'], b, l)
        o_a = sb_fn(q.reshape(b, l, W_A), kt, vt)
        heads_last = lambda a: jnp.transpose(a.reshape(b, H_A, D_A, l), (0, 3, 1, 2))
        k, v = heads_last(kt), heads_last(vt)
    else:
        q, k, v, bslab, cslab = _in_proj(x, p['norm_mix'], p['w_in'])
        o_a = sb_fn(q.reshape(b, l, W_A), k.reshape(b, l, W_A), v.reshape(b, l, W_A))
    o_b, s_hgrn = _hgrn(_pad_tokens(bslab.reshape(b, l, 4 * W_R), HG_SUB), jnp.log(lb), jnp.log1p(-lb),
                        p['hgrn_norm'], _pair_state_in(s_hgrn0), l)
    cslab = cslab.reshape(b, l, C_PROJ)
    o_c, s_rwkv = _rwkv(_pad_tokens(cslab, RW_CHUNK), shift0.reshape(b, 1, C_PROJ),
                        _quad_state_in(s_rwkv0), p, l)
    x1, qm = _out_q(x, o_a.reshape(b * l, W_A), o_b[:, :l].reshape(b * l, W_R),
                    o_c[:, :l].reshape(b * l, W_R), p['w_out'], p['norm_mem'], p['mem_wq'])
    x2 = _mem_attn(x1.reshape(b, l, D_MODEL), qm.reshape(b, l, D_MODEL), mem_k, mem_v, p['mem_wo'])
    x3 = _ffn(x2.reshape(b * l, D_MODEL), p['norm_ffn'], p['ffn_w_gate'], p['ffn_w_up'], p['ffn_w_down'],
              p['norm_ffn'] if final_gain is None else final_gain, final_gain is not None)
    return (x3, k.reshape(b, l, H_A, D_A), v.reshape(b, l, H_A, D_A), _pair_state_out(s_hgrn),
            _quad_state_out(s_rwkv), cslab[:, l - 1])


def kernel(x_prompt, x_sample, mem_prompt, cache_sb_k, cache_sb_v, page_table, state_hgrn, state_rwkv,
           state_rwkv_shift, cache_mem_k, cache_mem_v, norm_mix, w_in, sb_bias, hgrn_lb, hgrn_norm, rwkv_mu,
           rwkv_w0, rwkv_w_up, rwkv_a0, rwkv_a_up, rwkv_g_up, rwkv_k_k, rwkv_k_a, rwkv_r_k, rwkv_ln_w, rwkv_ln_b,
           w_out, norm_mem, mem_wq, mem_wk, mem_wv, mem_wo, norm_ffn, ffn_w_gate, ffn_w_up, ffn_w_down,
           final_norm):
    bp, lp, _ = x_prompt.shape
    bs, ls, _ = x_sample.shape
    depth = w_in.shape[0]
    n_pool = cache_sb_k.shape[1]
    lb_all = jnp.cumsum(jax.nn.softmax(hgrn_lb.astype(F32), axis=0), axis=0)
    lb_all = lb_all - lb_all[0]
    assert page_table.shape[1] % SB_PAGES == 0 and x_sample.shape[1] == N_DEC
    cache_kt = jnp.transpose(cache_sb_k, (0, 1, 3, 4, 2))
    cache_vt = jnp.transpose(cache_sb_v, (0, 1, 3, 4, 2))
    mem_tok = mem_prompt.reshape(bp * N_MEM, D_MODEL)
    xp = x_prompt.reshape(bp * lp, D_MODEL)
    xs = x_sample.reshape(bs * ls, D_MODEL)
    outs = [[] for _ in range(12)]
    for l in range(depth):
        bf = lambda w: w[l].astype(BF16)
        p = dict(norm_mix=norm_mix[l], w_in=bf(w_in), hgrn_norm=hgrn_norm[l], rwkv_mu=rwkv_mu[l],
                 rwkv_w0=rwkv_w0[l], rwkv_w_up=rwkv_w_up[l], rwkv_a0=rwkv_a0[l], rwkv_a_up=rwkv_a_up[l],
                 rwkv_g_up=rwkv_g_up[l], rwkv_k_k=rwkv_k_k[l], rwkv_k_a=rwkv_k_a[l], rwkv_r_k=rwkv_r_k[l],
                 rwkv_ln_w=rwkv_ln_w[l], rwkv_ln_b=rwkv_ln_b[l], w_out=bf(w_out), norm_mem=norm_mem[l],
                 mem_wq=bf(mem_wq), mem_wo=bf(mem_wo), norm_ffn=norm_ffn[l], ffn_w_gate=bf(ffn_w_gate),
                 ffn_w_up=bf(ffn_w_up), ffn_w_down=bf(ffn_w_down))
        gain = final_norm if l == depth - 1 else None
        bias = sb_bias[l]
        mk_p = _matmul(mem_tok, bf(mem_wk)).reshape(bp, N_MEM, D_MODEL)
        mv_p = _matmul(mem_tok, bf(mem_wv)).reshape(bp, N_MEM, D_MODEL)
        xp, kp, vp, hg_p, rw_p, sh_p = _trunk_layer(
            xp, bp, lp, functools.partial(_sb_prompt, bias=bias),
            jnp.zeros((bp, H_R, D_R, D_R), F32), jnp.zeros((bp, H_R, D_R, D_R), F32),
            jnp.zeros((bp, C_PROJ), F32), mk_p, mv_p, lb_all[l], p, gain, True)
        sb_fn = functools.partial(_sb_sample, bias=bias, cache_kt=cache_kt, cache_vt=cache_vt, layer=l,
                                  page_table=page_table)
        xs, ks, vs, hg_s, rw_s, sh_s = _trunk_layer(
            xs, bs, ls, sb_fn, state_hgrn[l], state_rwkv[l], state_rwkv_shift[l],
            cache_mem_k[l].reshape(bs, N_MEM, D_MODEL), cache_mem_v[l].reshape(bs, N_MEM, D_MODEL),
            lb_all[l], p, gain, False)
        for lst, val in zip(outs, (kp, vp, ks, vs, hg_p, hg_s, rw_p, rw_s, sh_p, sh_s,
                                   mk_p.reshape(bp, N_MEM, H_M, D_M), mv_p.reshape(bp, N_MEM, H_M, D_M))):
            lst.append(val)
    return (xp.reshape(bp, lp, D_MODEL), xs.reshape(bs, ls, D_MODEL)) + tuple(jnp.stack(o) for o in outs)
```

```python
import functools

import jax
import jax.numpy as jnp
from jax import lax
from jax.experimental import pallas as pl
from jax.experimental.pallas import tpu as pltpu

F32 = jnp.float32
BF16 = jnp.bfloat16

D_MODEL = 1024
H_A, D_A = 8, 64
W_A = H_A * D_A
H_R, D_R = 4, 64
W_R = H_R * D_R
H_M, D_M = 4, 256
N_MEM = 256
C_PROJ = 3 * W_R + 128
D_IN = 3 * W_A + 4 * W_R + C_PROJ
PAGE = 128
RMS_EPS = 1e-6
GN_EPS = 64e-5
L2_EPS = 1e-12

LANES = 128
VMEM_LIMIT = 56 * 1024 * 1024

NEG_BIG = -1e30


def _cparams(sem):
    return pltpu.CompilerParams(dimension_semantics=sem, vmem_limit_bytes=VMEM_LIMIT)


def _dot(a, b):
    return jnp.dot(a, b, preferred_element_type=F32)


def _dot_nt(a, b):
    return lax.dot_general(a, b, (((1,), (1,)), ((), ())), preferred_element_type=F32)


def _dot_tn(a, b):
    return lax.dot_general(a, b, (((0,), (0,)), ((), ())), preferred_element_type=F32)


def _split2(x):
    hi = x.astype(BF16)
    lo = (x - hi.astype(F32)).astype(BF16)
    return hi, lo


def _split3(x):
    hi = x.astype(BF16)
    r = x - hi.astype(F32)
    mid = r.astype(BF16)
    lo = (r - mid.astype(F32)).astype(BF16)
    return hi, mid, lo


def _rms(x, g):
    ms = jnp.mean(x * x, axis=-1, keepdims=True)
    return (x * lax.rsqrt(ms + RMS_EPS)) * g


def _log1p_exp(x):
    return jnp.log(1.0 + jnp.exp(-jnp.abs(x)))


def _const_spec(shape):
    nd = len(shape)
    return pl.BlockSpec(shape, lambda *_: (0,) * nd)


def _in_proj_kernel(x_ref, g_ref, w_ref, q_ref, k_ref, v_ref, b_ref, c_ref):
    hb = _rms(x_ref[...], g_ref[...]).astype(BF16)
    q_ref[...] = _dot(hb, w_ref[:, 0:W_A])
    k_ref[...] = _dot(hb, w_ref[:, W_A:2 * W_A])
    v_ref[...] = _dot(hb, w_ref[:, 2 * W_A:3 * W_A])
    b_ref[...] = _dot(hb, w_ref[:, 3 * W_A:3 * W_A + 4 * W_R])
    c_ref[...] = _dot(hb, w_ref[:, 3 * W_A + 4 * W_R:D_IN])


def _in_proj(x, g, w_bf):
    m = x.shape[0]
    tm = min(m, 512)
    widths = (W_A, W_A, W_A, 4 * W_R, C_PROJ)
    return pl.pallas_call(
        _in_proj_kernel,
        grid=(m // tm,),
        in_specs=[pl.BlockSpec((tm, D_MODEL), lambda i: (i, 0)),
                  _const_spec((1, D_MODEL)),
                  _const_spec((D_MODEL, D_IN))],
        out_specs=[pl.BlockSpec((tm, w), lambda i: (i, 0)) for w in widths],
        out_shape=[jax.ShapeDtypeStruct((m, w), F32) for w in widths],
        compiler_params=_cparams(("parallel",)),
        name="in_proj",
    )(x, g.reshape(1, D_MODEL), w_bf)


def _in_proj_t_kernel(x_ref, g_ref, w_ref, wkvt_ref, q_ref, kt_ref, vt_ref, b_ref, c_ref):
    hb = _rms(x_ref[...], g_ref[...]).astype(BF16)
    q_ref[...] = _dot(hb, w_ref[:, 0:W_A])
    kt_ref[0] = _dot_nt(wkvt_ref[0:W_A, :], hb)
    vt_ref[0] = _dot_nt(wkvt_ref[W_A:2 * W_A, :], hb)
    b_ref[...] = _dot(hb, w_ref[:, 3 * W_A:3 * W_A + 4 * W_R])
    c_ref[...] = _dot(hb, w_ref[:, 3 * W_A + 4 * W_R:D_IN])


def _in_proj_t(x, g, w_bf, b, l):
    m = x.shape[0]
    tm = min(l, 512)
    per_seq = l // tm
    wkvt = w_bf[:, W_A:3 * W_A].T
    row = lambda w: pl.BlockSpec((tm, w), lambda i: (i, 0))
    tr = pl.BlockSpec((1, W_A, tm), lambda i: (i // per_seq, 0, i % per_seq))
    return pl.pallas_call(
        _in_proj_t_kernel,
        grid=(m // tm,),
        in_specs=[row(D_MODEL), _const_spec((1, D_MODEL)), _const_spec((D_MODEL, D_IN)),
                  _const_spec((2 * W_A, D_MODEL))],
        out_specs=[row(W_A), tr, tr, row(4 * W_R), row(C_PROJ)],
        out_shape=[jax.ShapeDtypeStruct((m, W_A), F32),
                   jax.ShapeDtypeStruct((b, W_A, l), F32),
                   jax.ShapeDtypeStruct((b, W_A, l), F32),
                   jax.ShapeDtypeStruct((m, 4 * W_R), F32),
                   jax.ShapeDtypeStruct((m, C_PROJ), F32)],
        compiler_params=_cparams(("parallel",)),
        name="in_proj_t",
    )(x, g.reshape(1, D_MODEL), w_bf, wkvt)


def _matmul_kernel(x_ref, w_ref, o_ref):
    o_ref[...] = _dot(x_ref[...].astype(BF16), w_ref[...])


def _matmul(x, w_bf):
    m, k = x.shape
    n = w_bf.shape[1]
    tm = min(m, 512)
    return pl.pallas_call(
        _matmul_kernel,
        grid=(m // tm,),
        in_specs=[pl.BlockSpec((tm, k), lambda i: (i, 0)), _const_spec((k, n))],
        out_specs=pl.BlockSpec((tm, n), lambda i: (i, 0)),
        out_shape=jax.ShapeDtypeStruct((m, n), F32),
        compiler_params=_cparams(("parallel",)),
        name="matmul",
    )(x, w_bf)


def _out_q_kernel(x_ref, oa_ref, ob_ref, oc_ref, wo_ref, g_ref, wq_ref, x1_ref, qm_ref):
    x1 = (x_ref[...]
          + _dot(oa_ref[...].astype(BF16), wo_ref[0:W_A, :])
          + _dot(ob_ref[...].astype(BF16), wo_ref[W_A:W_A + W_R, :])
          + _dot(oc_ref[...].astype(BF16), wo_ref[W_A + W_R:W_A + 2 * W_R, :]))
    x1_ref[...] = x1
    qm_ref[...] = _dot(_rms(x1, g_ref[...]).astype(BF16), wq_ref[...])


def _out_q(x, oa, ob, oc, wo_bf, g, wq_bf):
    m = x.shape[0]
    tm = min(m, 512)
    row = lambda w: pl.BlockSpec((tm, w), lambda i: (i, 0))
    return pl.pallas_call(
        _out_q_kernel,
        grid=(m // tm,),
        in_specs=[row(D_MODEL), row(W_A), row(W_R), row(W_R),
                  _const_spec((D_MODEL, D_MODEL)), _const_spec((1, D_MODEL)),
                  _const_spec((D_MODEL, D_MODEL))],
        out_specs=[row(D_MODEL), row(D_MODEL)],
        out_shape=[jax.ShapeDtypeStruct((m, D_MODEL), F32)] * 2,
        compiler_params=_cparams(("parallel",)),
        name="out_q",
    )(x, oa, ob, oc, wo_bf, g.reshape(1, D_MODEL), wq_bf)


def _mem_attn_kernel(x_ref, q_ref, mk_ref, mv_ref, wo_ref, o_ref):
    acc = x_ref[0]
    for h in range(H_M):
        sl = slice(h * D_M, (h + 1) * D_M)
        qh = (q_ref[0, :, sl] * (D_M ** -0.5)).astype(BF16)
        s = _dot_nt(qh, mk_ref[0, :, sl].astype(BF16))
        e = jnp.exp(s - jnp.max(s, axis=-1, keepdims=True))
        p = e / jnp.sum(e, axis=-1, keepdims=True)
        oh = _dot(p.astype(BF16), mv_ref[0, :, sl].astype(BF16))
        acc = acc + _dot(oh.astype(BF16), wo_ref[sl, :])
    o_ref[0] = acc


def _mem_attn(x1, qm, mk, mv, wo_bf):
    b, l, _ = x1.shape
    tl = min(l, 512)
    tok = pl.BlockSpec((1, tl, D_MODEL), lambda i, j: (i, j, 0))
    mem = pl.BlockSpec((1, N_MEM, D_MODEL), lambda i, j: (i, 0, 0))
    return pl.pallas_call(
        _mem_attn_kernel,
        grid=(b, l // tl),
        in_specs=[tok, tok, mem, mem, _const_spec((D_MODEL, D_MODEL))],
        out_specs=tok,
        out_shape=jax.ShapeDtypeStruct((b, l, D_MODEL), F32),
        compiler_params=_cparams(("parallel", "parallel")),
        name="mem_attn",
    )(x1, qm, mk, mv, wo_bf)


FF_CHUNK = 256


def _ffn_kernel(x_ref, g_ref, wg_ref, wu_ref, wd_ref, gf_ref, o_ref, acc_ref, *, final_norm):
    x = x_ref[...]
    hb = _rms(x, g_ref[...]).astype(BF16)
    acc_ref[...] = x
    d_ff = wg_ref.shape[1]
    for c in range(d_ff // FF_CHUNK):
        sl = slice(c * FF_CHUNK, (c + 1) * FF_CHUNK)
        gate = _dot(hb, wg_ref[:, sl])
        up = _dot(hb, wu_ref[:, sl])
        a = (gate * jax.nn.sigmoid(gate)) * up
        acc_ref[...] += _dot(a.astype(BF16), wd_ref[sl, :])
    if final_norm:
        o_ref[...] = _rms(acc_ref[...], gf_ref[...])
    else:
        o_ref[...] = acc_ref[...]


def _ffn(x, g, wg_bf, wu_bf, wd_bf, gf, final_norm):
    m = x.shape[0]
    d_ff = wg_bf.shape[1]
    tm = min(m, 512)
    row = pl.BlockSpec((tm, D_MODEL), lambda i: (i, 0))
    return pl.pallas_call(
        functools.partial(_ffn_kernel, final_norm=final_norm),
        grid=(m // tm,),
        in_specs=[row, _const_spec((1, D_MODEL)), _const_spec((D_MODEL, d_ff)),
                  _const_spec((D_MODEL, d_ff)), _const_spec((d_ff, D_MODEL)),
                  _const_spec((1, D_MODEL))],
        out_specs=row,
        out_shape=jax.ShapeDtypeStruct((m, D_MODEL), F32),
        scratch_shapes=[pltpu.VMEM((tm, D_MODEL), F32)],
        compiler_params=_cparams(("parallel",)),
        name="ffn",
    )(x, g.reshape(1, D_MODEL), wg_bf, wu_bf, wd_bf, gf.reshape(1, D_MODEL))


def _log_terms(z):
    l1p = _log1p_exp(z)
    return -(jnp.maximum(z, 0.0) + l1p), jnp.minimum(z, 0.0) - l1p


def _sb_prompt_kernel(bias_ref, q_ref, kt_ref, vt_ref, u_ref, o_ref, qs_ref, kb_ref, vs_ref, carry_ref,
                      ls_ref, hl_ref, w_ref):
    i = pl.program_id(1)
    nkb = kt_ref.shape[2] // PAGE
    lane = lax.broadcasted_iota(jnp.int32, (PAGE, LANES), 1)
    row = lax.broadcasted_iota(jnp.int32, (PAGE, LANES), 0)
    even = lane < D_A
    even_row = row < D_A
    causal = lane < row

    @pl.when(i == 0)
    def _():
        for j in range(nkb):
            cols = slice(j * PAGE, (j + 1) * PAGE)
            kb_ref[j] = kt_ref[0, :, cols].astype(BF16)
            for p in range(H_A // 2):
                v2 = vt_ref[0, p * LANES:(p + 1) * LANES, cols]
                vs_ref[p, j, :, 0:PAGE] = jnp.where(even_row, v2, 0.0).astype(BF16)
                vs_ref[p, j, :, PAGE:2 * PAGE] = jnp.where(even_row, 0.0, v2).astype(BF16)

    for p in range(H_A // 2):
        q2 = q_ref[0, :, p * LANES:(p + 1) * LANES] * (D_A ** -0.5)
        qs_ref[p, 0:PAGE, :] = jnp.where(even, q2, 0.0).astype(BF16)
        qs_ref[p, PAGE:2 * PAGE, :] = jnp.where(even, 0.0, q2).astype(BF16)

    def blocks(js, diag):
        for n, j in enumerate(js):
            for p in range(H_A // 2):
                s2 = _dot(qs_ref[p], kb_ref[j, p * LANES:(p + 1) * LANES, :])
                for e in range(2):
                    h = 2 * p + e
                    z = s2[e * PAGE:(e + 1) * PAGE] + bias_ref[h]
                    ln, ls = _log_terms(z)
                    if diag and n == 0:
                        ln = jnp.where(causal, ln, 0.0)
                    hi, lo = _split2(ln)
                    ls_ref[n, h] = ls
                    hl_ref[n, h, :, 0:PAGE] = hi
                    hl_ref[n, h, :, PAGE:2 * PAGE] = lo
        for n, j in enumerate(js):
            for h in range(H_A):
                r = _dot(hl_ref[n, h], u_ref[...])
                between, tot = r[:, 0:LANES], r[:, LANES:2 * LANES]
                if diag and n == 0:
                    w = jnp.where(causal, jnp.exp(ls_ref[n, h] + between), 0.0)
                    carry_ref[h] = tot
                else:
                    c = carry_ref[h]
                    w = jnp.exp(ls_ref[n, h] + between + c)
                    carry_ref[h] = c + tot
                w_ref[n, h // 2, :, (h % 2) * PAGE:(h % 2 + 1) * PAGE] = w.astype(BF16)
        for p in range(H_A // 2):
            pv = _dot_nt(w_ref[0, p], vs_ref[p, js[0]])
            for n in range(1, len(js)):
                pv = pv + _dot_nt(w_ref[n, p], vs_ref[p, js[n]])
            if diag:
                o_ref[0, :, p * LANES:(p + 1) * LANES] = pv
            else:
                o_ref[0, :, p * LANES:(p + 1) * LANES] += pv

    blocks([i], True)
    odd = i % 2

    @pl.when(odd == 1)
    def _():
        blocks([i - 1], False)

    def older(t, _):
        j = i - 1 - odd - 2 * t
        blocks([j, j - 1], False)
        return 0

    lax.fori_loop(0, i // 2, older, 0)


def _sb_tri():
    j = lax.broadcasted_iota(jnp.int32, (2 * PAGE, 2 * LANES), 0) % PAGE
    s = lax.broadcasted_iota(jnp.int32, (2 * PAGE, 2 * LANES), 1)
    return ((j > s) | (s >= LANES)).astype(BF16)


def _sb_prompt(q, kt, vt, bias):
    b, l, _ = q.shape
    nq = l // PAGE
    tok = pl.BlockSpec((1, PAGE, W_A), lambda i, j: (i, j, 0))
    seq = pl.BlockSpec((1, W_A, l), lambda i, j: (i, 0, 0))
    return pl.pallas_call(
        _sb_prompt_kernel,
        grid=(b, nq),
        in_specs=[pl.BlockSpec(memory_space=pltpu.SMEM), tok, seq, seq,
                  _const_spec((2 * PAGE, 2 * LANES))],
        out_specs=tok,
        out_shape=jax.ShapeDtypeStruct((b, l, W_A), F32),
        scratch_shapes=[pltpu.VMEM((H_A // 2, 2 * PAGE, LANES), BF16),
                        pltpu.VMEM((nq, W_A, PAGE), BF16),
                        pltpu.VMEM((H_A // 2, nq, LANES, 2 * PAGE), BF16),
                        pltpu.VMEM((H_A, PAGE, LANES), F32),
                        pltpu.VMEM((2, H_A, PAGE, LANES), F32),
                        pltpu.VMEM((2, H_A, PAGE, 2 * PAGE), BF16),
                        pltpu.VMEM((2, H_A // 2, PAGE, 2 * PAGE), BF16)],
        compiler_params=_cparams(("parallel", "arbitrary")),
        name="sb_prompt",
    )(bias, q, kt, vt, _sb_tri())


N_DEC = 4


SB_PAGES = 16
N_QROW = N_DEC * H_A


def _sb_sample_kernel(pt_ref, q_ref, br_ref, kn_ref, vn_ref, *rest):
    del pt_ref
    kp_refs, vp_refs = rest[:SB_PAGES], rest[SB_PAGES:2 * SB_PAGES]
    u_ref, o_ref, acc_ref, carry_ref = rest[2 * SB_PAGES:]
    j = pl.program_id(1)

    def logits(kt):
        z = _dot(q_ref[0], kt.astype(BF16)) + br_ref[...]
        return _log_terms(z)

    def suffix(ln):
        hi, lo = _split2(ln)
        r = _dot(jnp.concatenate([hi, lo], axis=1), u_ref[...])
        return r[:, 0:LANES], r[:, LANES:2 * LANES]

    @pl.when(j == 0)
    def _():
        key = lax.broadcasted_iota(jnp.int32, (N_QROW, LANES), 1)
        tok = lax.broadcasted_iota(jnp.int32, (N_QROW, LANES), 0) // H_A
        visible = key < tok
        ln, ls = logits(kn_ref[0])
        ln = jnp.where(visible, ln, 0.0)
        between, tot = suffix(ln)
        w = jnp.where(visible, jnp.exp(ls + between), 0.0)
        carry_ref[...] = tot
        acc_ref[...] = _dot_nt(w.astype(BF16), vn_ref[0].astype(BF16))

    @pl.when(j > 0)
    def _():
        terms = [logits(kp_refs[i][0, 0].reshape(W_A, PAGE)) for i in range(SB_PAGES)]
        sums = [suffix(ln) for ln, _ in terms]
        c = carry_ref[...]
        ws = []
        for (_, ls), (between, tot) in zip(terms, sums):
            ws.append(jnp.exp(ls + between + c).astype(BF16))
            c = c + tot
        carry_ref[...] = c
        acc = acc_ref[...]
        for i in range(SB_PAGES):
            acc = acc + _dot_nt(ws[i], vp_refs[i][0, 0].reshape(W_A, PAGE).astype(BF16))
        acc_ref[...] = acc

    @pl.when(j == pl.num_programs(1) - 1)
    def _():
        head = lax.broadcasted_iota(jnp.int32, (H_A, W_A), 0)
        lane_head = lax.broadcasted_iota(jnp.int32, (H_A, W_A), 1) // D_A
        for t in range(N_DEC):
            rows = acc_ref[t * H_A:(t + 1) * H_A, :]
            o_ref[0, t:t + 1, :] = jnp.sum(jnp.where(head == lane_head, rows, 0.0), axis=0, keepdims=True)


def _sb_sample(q, k, v, bias, cache_kt, cache_vt, layer, page_table):
    b = q.shape[0]
    n_pages = page_table.shape[1]
    eye = jnp.eye(H_A, dtype=F32)
    q4 = q.reshape(b, N_DEC, H_A, D_A) * (D_A ** -0.5)
    qbd = jnp.einsum('bthd,hg->btghd', q4, eye).reshape(b, N_QROW, W_A).astype(BF16)
    bias_rows = jnp.broadcast_to(jnp.tile(bias, N_DEC)[:, None], (N_QROW, LANES))
    new_t = lambda a: jnp.pad(jnp.swapaxes(a, 1, 2), ((0, 0), (0, 0), (0, PAGE - N_DEC)))

    def page_map(slot):
        def index(i, j, pt):
            return (layer, pt[i, n_pages - 1 - ((jnp.maximum(j, 1) - 1) * SB_PAGES + slot)], 0, 0, 0)
        return pl.BlockSpec((1, 1, H_A, D_A, PAGE), index)

    per_b = lambda shape: pl.BlockSpec((1,) + shape, lambda i, j, pt: (i, 0, 0))
    pages = [page_map(s) for s in range(SB_PAGES)]
    grid_spec = pltpu.PrefetchScalarGridSpec(
        num_scalar_prefetch=1,
        grid=(b, n_pages // SB_PAGES + 1),
        in_specs=[per_b((N_QROW, W_A)),
                  pl.BlockSpec((N_QROW, LANES), lambda i, j, pt: (0, 0)),
                  per_b((W_A, PAGE)), per_b((W_A, PAGE))] + pages + pages
                 + [pl.BlockSpec((2 * PAGE, 2 * LANES), lambda i, j, pt: (0, 0))],
        out_specs=per_b((N_DEC, W_A)),
        scratch_shapes=[pltpu.VMEM((N_QROW, W_A), F32), pltpu.VMEM((N_QROW, LANES), F32)],
    )
    return pl.pallas_call(
        _sb_sample_kernel,
        grid_spec=grid_spec,
        out_shape=jax.ShapeDtypeStruct((b, N_DEC, W_A), F32),
        compiler_params=_cparams(("parallel", "arbitrary")),
        name="sb_sample",
    )(page_table, qbd, bias_rows, new_t(k), new_t(v), *([cache_kt] * SB_PAGES), *([cache_vt] * SB_PAGES),
      _sb_tri())


def _mm(a, b, kind="nn", passes=1):
    dot = {"nn": _dot, "nt": _dot_nt, "tn": _dot_tn}[kind]
    if passes == 1:
        return dot(a.astype(BF16), b.astype(BF16))
    a_hi, a_lo = _split2(a)
    if passes == 2:
        b_hi = b.astype(BF16)
        return dot(a_hi, b_hi) + dot(a_lo, b_hi)
    b_hi, b_lo = _split2(b)
    return dot(a_hi, b_hi) + (dot(a_hi, b_lo) + dot(a_lo, b_hi))


def _head_ones(n):
    r = lax.broadcasted_iota(jnp.int32, (n, n), 0) // D_R
    c = lax.broadcasted_iota(jnp.int32, (n, n), 1) // D_R
    return (r == c).astype(BF16)


def _head_sum(x, ones):
    hi, mid, lo = _split3(x)
    return _dot(hi, ones) + (_dot(mid, ones) + _dot(lo, ones))


def _softplus(x):
    return jnp.maximum(x, 0.0) + _log1p_exp(x)


HG_SUB = 16


def _hgrn_kernel(q_ref, f_ref, i_ref, g_ref, la_ref, l1_ref, gn_ref, s0_ref, bt_ref, ones_ref,
                 o_ref, sf_ref, s_ref, *, valid):
    c = pl.program_id(1)

    @pl.when(c == 0)
    def _():
        s_ref[...] = s0_ref[0]

    tc = q_ref.shape[1]
    bq = q_ref[0]
    qs = bq * jax.nn.sigmoid(bq)
    bf = f_ref[0]
    la = la_ref[...]
    l1p = _log1p_exp(bf)
    lb = l1_ref[...] + (jnp.minimum(bf, 0.0) - l1p)
    lf = jnp.maximum(la, lb) + _log1p_exp(la - lb)
    kk = jnp.exp(l1_ref[...] - (jnp.maximum(bf, 0.0) + l1p))
    if valid < tc:
        live = lax.broadcasted_iota(jnp.int32, (tc, W_R), 0) < valid
        lf = jnp.where(live, lf, 0.0)
        kk = jnp.where(live, kk, 0.0)
    hi, mid, lo = _split3(lf)
    cum = _dot(bt_ref[...], hi) + (_dot(bt_ref[...], mid) + _dot(bt_ref[...], lo))
    iv = i_ref[0]

    trow = lax.broadcasted_iota(jnp.int32, (HG_SUB, LANES), 0)
    r2 = lax.broadcasted_iota(jnp.int32, (LANES, LANES), 0) // D_R
    c2 = lax.broadcasted_iota(jnp.int32, (LANES, LANES), 1) // D_R
    same_head = r2 == c2
    ones_pair = ones_ref[0:LANES, 0:LANES]

    n_sub = tc // HG_SUB
    units = [(sc, p) for sc in range(n_sub) for p in range(2)]
    part = lambda x, sc, p: x[sc * HG_SUB:(sc + 1) * HG_SUB, p * LANES:(p + 1) * LANES]

    scores, upd, q_dec, decay, ivs = {}, {}, {}, {}, {}
    for u in units:
        cumc, qc, kc, ic = (part(x, *u) for x in (cum, qs, kk, iv))
        last = cumc[HG_SUB - 1:HG_SUB]
        ps = []
        for j in range(HG_SUB):
            d = jnp.where(trow >= j, cumc - cumc[j:j + 1], NEG_BIG)
            ps.append(jnp.exp(d) * qc * kc[j:j + 1])
        scores[u] = _dot(jnp.concatenate(ps, axis=0).astype(BF16), ones_pair)
        upd[u] = _dot_tn(ic.astype(BF16), (kc * jnp.exp(last - cumc)).astype(BF16))
        q_dec[u] = (qc * jnp.exp(cumc)).astype(BF16)
        decay[u] = jnp.exp(last)
        ivs[u] = ic

    s = [s_ref[0], s_ref[1]]
    inter = {}
    for u in units:
        p = u[1]
        inter[u] = _dot_nt(q_dec[u], s[p].astype(BF16))
        s[p] = s[p] * decay[u] + jnp.where(same_head, upd[u], 0.0)
    s_ref[0] = s[0]
    s_ref[1] = s[1]

    outs = {}
    for u in units:
        intra = scores[u][0:HG_SUB] * ivs[u][0:1]
        for j in range(1, HG_SUB):
            intra = intra + scores[u][j * HG_SUB:(j + 1) * HG_SUB] * ivs[u][j:j + 1]
        outs[u] = inter[u] + intra
    o = jnp.concatenate([jnp.concatenate([outs[(sc, p)] for p in range(2)], axis=1) for sc in range(n_sub)],
                        axis=0)
    ms = _head_sum(o * o, ones_ref[...]) * (1.0 / D_R)
    bg = g_ref[0]
    o_ref[0] = (o * lax.rsqrt(ms + RMS_EPS)) * gn_ref[...] * (bg * jax.nn.sigmoid(bg))

    @pl.when(c == pl.num_programs(1) - 1)
    def _():
        sf_ref[0] = s_ref[...]


def _hgrn(bslab, log_lb, log1m_lb, gnorm, s0bd, valid):
    b, l, _ = bslab.shape
    tc = min(l, 128)
    col = lambda k: pl.BlockSpec((1, tc, W_R), lambda i, c: (i, c, k))
    par = _const_spec((1, W_R))
    st = pl.BlockSpec((1, 2, LANES, LANES), lambda i, c: (i, 0, 0, 0))
    t = lax.broadcasted_iota(jnp.int32, (tc, tc), 0)
    s = lax.broadcasted_iota(jnp.int32, (tc, tc), 1)
    blk_tri = ((t // HG_SUB == s // HG_SUB) & (s <= t)).astype(BF16)
    return pl.pallas_call(
        functools.partial(_hgrn_kernel, valid=valid),
        grid=(b, l // tc),
        in_specs=[col(0), col(1), col(2), col(3), par, par, par, st,
                  _const_spec((tc, tc)), _const_spec((W_R, W_R))],
        out_specs=[pl.BlockSpec((1, tc, W_R), lambda i, c: (i, c, 0)), st],
        out_shape=[jax.ShapeDtypeStruct((b, l, W_R), F32),
                   jax.ShapeDtypeStruct((b, 2, LANES, LANES), F32)],
        scratch_shapes=[pltpu.VMEM((2, LANES, LANES), F32)],
        compiler_params=_cparams(("parallel", "arbitrary")),
        name="hgrn",
    )(bslab, bslab, bslab, bslab, log_lb.reshape(1, W_R), log1m_lb.reshape(1, W_R),
      gnorm.reshape(1, W_R), s0bd, blk_tri, _head_ones(W_R))


def _pair_state_in(s):
    b = s.shape[0]
    st = jnp.swapaxes(s, 2, 3).reshape(b, 2, 2, D_R, D_R)
    return jnp.einsum('bpevk,ef->bpevfk', st, jnp.eye(2, dtype=s.dtype)).reshape(b, 2, LANES, LANES)


def _pair_state_out(sbd):
    b = sbd.shape[0]
    s6 = sbd.reshape(b, 2, 2, D_R, 2, D_R)
    st = jnp.stack([s6[:, :, 0, :, 0, :], s6[:, :, 1, :, 1, :]], axis=2)
    return jnp.swapaxes(st.reshape(b, H_R, D_R, D_R), 2, 3)


RW_CHUNK = 64
RW_N = H_R * RW_CHUNK


RW_SEQS = 4


def _each(f, *lists):
    return [f(*args) for args in zip(*lists)]


def _rwkv_kernel(c_ref, sh0_ref, s0_ref, mu_ref, w0_ref, wup_ref, a0_ref, aup_ref, gup_ref,
                 kk_ref, ka_ref, rk_ref, lnw_ref, lnb_ref, tri_ref, ones_ref,
                 o_ref, sf_ref, prev_ref, s_ref, *, valid, passes):
    c = pl.program_id(1)

    @pl.when(c == 0)
    def _():
        prev_ref[...] = sh0_ref[...]
        s_ref[...] = s0_ref[...]

    seqs = list(range(RW_SEQS))
    cb = [c_ref[n] for n in seqs]
    tok = lax.broadcasted_iota(jnp.int32, (RW_CHUNK, C_PROJ), 0)
    shifted = [jnp.where(tok == 0, prev_ref[n], pltpu.roll(cb[n], 1, axis=0)) for n in seqs]
    for n in seqs:
        prev_ref[n] = cb[n][RW_CHUNK - 1:RW_CHUNK]
    xm = _each(lambda x, sh: x + (sh - x) * mu_ref[...], cb, shifted)
    r = [x[:, 0:W_R] for x in xm]
    k = [x[:, W_R:2 * W_R] for x in xm]
    v = [x[:, 2 * W_R:3 * W_R] for x in xm]
    tail = [x[:, 3 * W_R:C_PROJ] for x in xm]
    ones = ones_ref[...]

    w_log = _each(lambda t: -_softplus(-(w0_ref[...] + _mm(jnp.tanh(t), wup_ref[...]))) - 0.5, tail)
    lw = [-jnp.exp(x) for x in w_log]
    a = _each(lambda t: jax.nn.sigmoid(a0_ref[...] + _mm(t, aup_ref[...])), tail)
    g = _each(lambda t: _mm(jax.nn.sigmoid(t), gup_ref[...]), tail)
    kk = [x * kk_ref[...] for x in k]
    kk = _each(lambda x: x / jnp.maximum(jnp.sqrt(_head_sum(x * x, ones)), L2_EPS), kk)
    k = _each(lambda x, aa: x * (1.0 + (aa - 1.0) * ka_ref[...]), k, a)
    av = [-x for x in kk]
    bv = _each(lambda x, aa: x * aa, kk, a)
    if valid < RW_CHUNK:
        live = lax.broadcasted_iota(jnp.int32, (RW_CHUNK, W_R), 0) < valid
        zero = lambda xs: [jnp.where(live, x, 0.0) for x in xs]
        lw, av, bv, k, v = zero(lw), zero(av), zero(bv), zero(k), zero(v)

    def cumsum(x):
        hi, mid, lo = _split3(x)
        return _dot(tri_ref[...], hi) + (_dot(tri_ref[...], mid) + _dot(tri_ref[...], lo))

    cl = _each(cumsum, lw)
    last = [x[RW_CHUNK - 1:RW_CHUNK] for x in cl]
    e_neg = [jnp.exp(-x) for x in cl]
    e_end = _each(lambda l, x: jnp.exp(l - x), last, cl)

    lane_head = lax.broadcasted_iota(jnp.int32, (RW_CHUNK, W_R), 1) // D_R

    def stack(x):
        return jnp.concatenate([jnp.where(lane_head == h, x, 0.0) for h in range(H_R)], axis=0)

    m_a = _each(lambda x, c_, l_: stack(x * jnp.exp(c_ - l_)), av, cl, lw)
    m_b = _each(lambda x, e: stack(x * e), bv, e_neg)
    m_k = _each(lambda x, e: stack(x * e), k, e_neg)
    m_r = _each(lambda x, c_: stack(x * jnp.exp(c_)), r, cl)
    m_v = _each(stack, v)
    m_bh = _each(lambda x, e: stack(x * e), bv, e_end)
    m_kh = _each(lambda x, e: stack(x * e), k, e_end)

    row = lax.broadcasted_iota(jnp.int32, (RW_N, RW_N), 0)
    col = lax.broadcasted_iota(jnp.int32, (RW_N, RW_N), 1)
    mm = functools.partial(_mm, passes=passes)
    nt = functools.partial(mm, kind="nt")
    tn = functools.partial(mm, kind="tn")
    add = lambda x, y: x + y
    t_ab = _each(lambda x, y: jnp.where(row > col, nt(x, y), 0.0), m_a, m_b)
    t_ak = _each(lambda x, y: jnp.where(row > col, nt(x, y), 0.0), m_a, m_k)
    t_rb = _each(lambda x, y: jnp.where(row >= col, nt(x, y), 0.0), m_r, m_b)
    t_rk = _each(lambda x, y: jnp.where(row >= col, nt(x, y), 0.0), m_r, m_k)

    inv = [jnp.where(row == col, 1.0, t) for t in t_ab]
    pw = t_ab
    for _ in range(5):
        pw = _each(mm, pw, pw)
        inv = _each(add, inv, _each(mm, inv, pw))

    w1 = _each(mm, inv, m_a)
    w2 = _each(mm, inv, _each(mm, t_ak, m_v))
    rm = _each(add, m_r, _each(mm, t_rb, w1))
    yc = _each(add, _each(mm, t_rb, w2), _each(mm, t_rk, m_v))
    pm = _each(lambda l, x, y: jnp.where(row == col, jnp.exp(l), 0.0) + tn(x, y), last, w1, m_bh)
    qm = _each(add, _each(tn, w2, m_bh), _each(tn, m_v, m_kh))

    s = [s_ref[n] for n in seqs]
    y_st = _each(add, _each(nt, rm, s), yc)
    s_new = _each(lambda x, p_, q_: _mm(x, p_, passes=max(passes, 2)) + q_, s, pm, qm)
    for n in seqs:
        s_ref[n] = s_new[n]
    y = [(t[0:RW_CHUNK] + t[RW_CHUNK:2 * RW_CHUNK]) + (t[2 * RW_CHUNK:3 * RW_CHUNK] + t[3 * RW_CHUNK:])
         for t in y_st]

    mean = [_head_sum(x, ones) * (1.0 / D_R) for x in y]
    yc0 = _each(lambda x, m: x - m, y, mean)
    var = [_head_sum(x * x, ones) * (1.0 / D_R) for x in yc0]
    yn = _each(lambda x, vv: x * lax.rsqrt(vv + GN_EPS) * lnw_ref[...] + lnb_ref[...], yc0, var)
    bonus = _each(lambda rr, kk_, vv: _head_sum(rr * kk_ * rk_ref[...], ones) * vv, r, k, v)
    for n in seqs:
        o_ref[n] = (yn[n] + bonus[n]) * g[n]

    @pl.when(c == pl.num_programs(1) - 1)
    def _():
        sf_ref[...] = s_ref[...]


def _rwkv(cslab, shift0, s0bd, p, valid, passes=1):
    b, l, _ = cslab.shape
    row = lambda a: a.reshape(1, -1)
    pad_up = lambda w, off: jnp.zeros((LANES, W_R), F32).at[off:off + w.shape[0]].set(w).astype(BF16)
    t = lax.broadcasted_iota(jnp.int32, (RW_CHUNK, RW_CHUNK), 0)
    s = lax.broadcasted_iota(jnp.int32, (RW_CHUNK, RW_CHUNK), 1)
    consts = [row(p['rwkv_mu']), row(p['rwkv_w0']), pad_up(p['rwkv_w_up'], 0), row(p['rwkv_a0']),
              pad_up(p['rwkv_a_up'], 32), pad_up(p['rwkv_g_up'], 64), row(p['rwkv_k_k']),
              row(p['rwkv_k_a']), row(p['rwkv_r_k']), row(p['rwkv_ln_w']), row(p['rwkv_ln_b']),
              (s <= t).astype(BF16), _head_ones(W_R)]
    assert b % RW_SEQS == 0
    st = pl.BlockSpec((RW_SEQS, RW_N, RW_N), lambda i, c: (i, 0, 0))
    return pl.pallas_call(
        functools.partial(_rwkv_kernel, valid=valid, passes=passes),
        grid=(b // RW_SEQS, l // RW_CHUNK),
        in_specs=[pl.BlockSpec((RW_SEQS, RW_CHUNK, C_PROJ), lambda i, c: (i, c, 0)),
                  pl.BlockSpec((RW_SEQS, 1, C_PROJ), lambda i, c: (i, 0, 0)), st]
                 + [_const_spec(a.shape) for a in consts],
        out_specs=[pl.BlockSpec((RW_SEQS, RW_CHUNK, W_R), lambda i, c: (i, c, 0)), st],
        out_shape=[jax.ShapeDtypeStruct((b, l, W_R), F32),
                   jax.ShapeDtypeStruct((b, RW_N, RW_N), F32)],
        scratch_shapes=[pltpu.VMEM((RW_SEQS, 1, C_PROJ), F32), pltpu.VMEM((RW_SEQS, RW_N, RW_N), F32)],
        compiler_params=_cparams(("parallel", "arbitrary")),
        name="rwkv",
    )(cslab, shift0, s0bd, *consts)


def _quad_state_in(s):
    b = s.shape[0]
    return jnp.einsum('bhij,hg->bhigj', s, jnp.eye(H_R, dtype=s.dtype)).reshape(b, RW_N, RW_N)


def _quad_state_out(sbd):
    b = sbd.shape[0]
    s5 = sbd.reshape(b, H_R, D_R, H_R, D_R)
    return jnp.stack([s5[:, h, :, h, :] for h in range(H_R)], axis=1)


def _pad_tokens(a, multiple):
    pad = (-a.shape[1]) % multiple
    return a if pad == 0 else jnp.pad(a, ((0, 0), (0, pad), (0, 0)))


def _trunk_layer(x, b, l, sb_fn, s_hgrn0, s_rwkv0, shift0, mem_k, mem_v, lb, p, final_gain, prompt):
    if prompt:
        q, kt, vt, bslab, cslab = _in_proj_t(x, p['norm_mix'], p['w_in'], b, l)
        o_a = sb_fn(q.reshape(b, l, W_A), kt, vt)
        heads_last = lambda a: jnp.transpose(a.reshape(b, H_A, D_A, l), (0, 3, 1, 2))
        k, v = heads_last(kt), heads_last(vt)
    else:
        q, k, v, bslab, cslab = _in_proj(x, p['norm_mix'], p['w_in'])
        o_a = sb_fn(q.reshape(b, l, W_A), k.reshape(b, l, W_A), v.reshape(b, l, W_A))
    o_b, s_hgrn = _hgrn(_pad_tokens(bslab.reshape(b, l, 4 * W_R), HG_SUB), jnp.log(lb), jnp.log1p(-lb),
                        p['hgrn_norm'], _pair_state_in(s_hgrn0), l)
    cslab = cslab.reshape(b, l, C_PROJ)
    o_c, s_rwkv = _rwkv(_pad_tokens(cslab, RW_CHUNK), shift0.reshape(b, 1, C_PROJ),
                        _quad_state_in(s_rwkv0), p, l)
    x1, qm = _out_q(x, o_a.reshape(b * l, W_A), o_b[:, :l].reshape(b * l, W_R),
                    o_c[:, :l].reshape(b * l, W_R), p['w_out'], p['norm_mem'], p['mem_wq'])
    x2 = _mem_attn(x1.reshape(b, l, D_MODEL), qm.reshape(b, l, D_MODEL), mem_k, mem_v, p['mem_wo'])
    x3 = _ffn(x2.reshape(b * l, D_MODEL), p['norm_ffn'], p['ffn_w_gate'], p['ffn_w_up'], p['ffn_w_down'],
              p['norm_ffn'] if final_gain is None else final_gain, final_gain is not None)
    return (x3, k.reshape(b, l, H_A, D_A), v.reshape(b, l, H_A, D_A), _pair_state_out(s_hgrn),
            _quad_state_out(s_rwkv), cslab[:, l - 1])


def kernel(x_prompt, x_sample, mem_prompt, cache_sb_k, cache_sb_v, page_table, state_hgrn, state_rwkv,
           state_rwkv_shift, cache_mem_k, cache_mem_v, norm_mix, w_in, sb_bias, hgrn_lb, hgrn_norm, rwkv_mu,
           rwkv_w0, rwkv_w_up, rwkv_a0, rwkv_a_up, rwkv_g_up, rwkv_k_k, rwkv_k_a, rwkv_r_k, rwkv_ln_w, rwkv_ln_b,
           w_out, norm_mem, mem_wq, mem_wk, mem_wv, mem_wo, norm_ffn, ffn_w_gate, ffn_w_up, ffn_w_down,
           final_norm):
    bp, lp, _ = x_prompt.shape
    bs, ls, _ = x_sample.shape
    depth = w_in.shape[0]
    n_pool = cache_sb_k.shape[1]
    lb_all = jnp.cumsum(jax.nn.softmax(hgrn_lb.astype(F32), axis=0), axis=0)
    lb_all = lb_all - lb_all[0]
    assert page_table.shape[1] % SB_PAGES == 0 and x_sample.shape[1] == N_DEC
    cache_kt = jnp.transpose(cache_sb_k, (0, 1, 3, 4, 2))
    cache_vt = jnp.transpose(cache_sb_v, (0, 1, 3, 4, 2))
    mem_tok = mem_prompt.reshape(bp * N_MEM, D_MODEL)
    xp = x_prompt.reshape(bp * lp, D_MODEL)
    xs = x_sample.reshape(bs * ls, D_MODEL)
    outs = [[] for _ in range(12)]
    for l in range(depth):
        bf = lambda w: w[l].astype(BF16)
        p = dict(norm_mix=norm_mix[l], w_in=bf(w_in), hgrn_norm=hgrn_norm[l], rwkv_mu=rwkv_mu[l],
                 rwkv_w0=rwkv_w0[l], rwkv_w_up=rwkv_w_up[l], rwkv_a0=rwkv_a0[l], rwkv_a_up=rwkv_a_up[l],
                 rwkv_g_up=rwkv_g_up[l], rwkv_k_k=rwkv_k_k[l], rwkv_k_a=rwkv_k_a[l], rwkv_r_k=rwkv_r_k[l],
                 rwkv_ln_w=rwkv_ln_w[l], rwkv_ln_b=rwkv_ln_b[l], w_out=bf(w_out), norm_mem=norm_mem[l],
                 mem_wq=bf(mem_wq), mem_wo=bf(mem_wo), norm_ffn=norm_ffn[l], ffn_w_gate=bf(ffn_w_gate),
                 ffn_w_up=bf(ffn_w_up), ffn_w_down=bf(ffn_w_down))
        gain = final_norm if l == depth - 1 else None
        bias = sb_bias[l]
        mk_p = _matmul(mem_tok, bf(mem_wk)).reshape(bp, N_MEM, D_MODEL)
        mv_p = _matmul(mem_tok, bf(mem_wv)).reshape(bp, N_MEM, D_MODEL)
        xp, kp, vp, hg_p, rw_p, sh_p = _trunk_layer(
            xp, bp, lp, functools.partial(_sb_prompt, bias=bias),
            jnp.zeros((bp, H_R, D_R, D_R), F32), jnp.zeros((bp, H_R, D_R, D_R), F32),
            jnp.zeros((bp, C_PROJ), F32), mk_p, mv_p, lb_all[l], p, gain, True)
        sb_fn = functools.partial(_sb_sample, bias=bias, cache_kt=cache_kt, cache_vt=cache_vt, layer=l,
                                  page_table=page_table)
        xs, ks, vs, hg_s, rw_s, sh_s = _trunk_layer(
            xs, bs, ls, sb_fn, state_hgrn[l], state_rwkv[l], state_rwkv_shift[l],
            cache_mem_k[l].reshape(bs, N_MEM, D_MODEL), cache_mem_v[l].reshape(bs, N_MEM, D_MODEL),
            lb_all[l], p, gain, False)
        for lst, val in zip(outs, (kp, vp, ks, vs, hg_p, hg_s, rw_p, rw_s, sh_p, sh_s,
                                   mk_p.reshape(bp, N_MEM, H_M, D_M), mv_p.reshape(bp, N_MEM, H_M, D_M))):
            lst.append(val)
    return (xp.reshape(bp, lp, D_MODEL), xs.reshape(bs, ls, D_MODEL)) + tuple(jnp.stack(o) for o in outs)
```

```python
import functools

import jax
import jax.numpy as jnp
from jax import lax
from jax.experimental import pallas as pl
from jax.experimental.pallas import tpu as pltpu
import numpy as np

F32 = jnp.float32
BF16 = jnp.bfloat16

D_MODEL = 1024
H_A, D_A = 8, 64
W_A = H_A * D_A
H_R, D_R = 4, 64
W_R = H_R * D_R
H_M, D_M = 4, 256
N_MEM = 256
C_PROJ = 3 * W_R + 128
D_IN = 3 * W_A + 4 * W_R + C_PROJ
PAGE = 128
RMS_EPS = 1e-6
GN_EPS = 64e-5
L2_EPS = 1e-12

LANES = 128
VMEM_LIMIT = 56 * 1024 * 1024

NEG_BIG = -1e30


def _cparams(sem):
    return pltpu.CompilerParams(dimension_semantics=sem, vmem_limit_bytes=VMEM_LIMIT)


def _dot(a, b):
    return jnp.dot(a, b, preferred_element_type=F32)


def _dot_nt(a, b):
    return lax.dot_general(a, b, (((1,), (1,)), ((), ())), preferred_element_type=F32)


def _dot_tn(a, b):
    return lax.dot_general(a, b, (((0,), (0,)), ((), ())), preferred_element_type=F32)


def _split2(x):
    hi = x.astype(BF16)
    lo = (x - hi.astype(F32)).astype(BF16)
    return hi, lo


def _split3(x):
    hi = x.astype(BF16)
    r = x - hi.astype(F32)
    mid = r.astype(BF16)
    lo = (r - mid.astype(F32)).astype(BF16)
    return hi, mid, lo


def _rms(x, g):
    ms = jnp.mean(x * x, axis=-1, keepdims=True)
    return (x * lax.rsqrt(ms + RMS_EPS)) * g


def _log1p_exp(x):
    return jnp.log(1.0 + jnp.exp(-jnp.abs(x)))


def _const_spec(shape):
    nd = len(shape)
    return pl.BlockSpec(shape, lambda *_: (0,) * nd)


def _in_proj_kernel(x_ref, g_ref, w_ref, q_ref, k_ref, v_ref, b_ref, c_ref):
    hb = _rms(x_ref[...], g_ref[...]).astype(BF16)
    q_ref[...] = _dot(hb, w_ref[:, 0:W_A])
    k_ref[...] = _dot(hb, w_ref[:, W_A:2 * W_A])
    v_ref[...] = _dot(hb, w_ref[:, 2 * W_A:3 * W_A])
    b_ref[...] = _dot(hb, w_ref[:, 3 * W_A:3 * W_A + 4 * W_R])
    c_ref[...] = _dot(hb, w_ref[:, 3 * W_A + 4 * W_R:D_IN])


def _in_proj(x, g, w_bf):
    m = x.shape[0]
    tm = min(m, 512)
    widths = (W_A, W_A, W_A, 4 * W_R, C_PROJ)
    return pl.pallas_call(
        _in_proj_kernel,
        grid=(m // tm,),
        in_specs=[pl.BlockSpec((tm, D_MODEL), lambda i: (i, 0)),
                  _const_spec((1, D_MODEL)),
                  _const_spec((D_MODEL, D_IN))],
        out_specs=[pl.BlockSpec((tm, w), lambda i: (i, 0)) for w in widths],
        out_shape=[jax.ShapeDtypeStruct((m, w), F32) for w in widths],
        compiler_params=_cparams(("parallel",)),
        name="in_proj",
    )(x, g.reshape(1, D_MODEL), w_bf)


def _in_proj_t_kernel(x_ref, g_ref, w_ref, wkvt_ref, q_ref, kt_ref, vt_ref, b_ref, c_ref):
    hb = _rms(x_ref[...], g_ref[...]).astype(BF16)
    q_ref[...] = _dot(hb, w_ref[:, 0:W_A])
    kt_ref[0] = _dot_nt(wkvt_ref[0:W_A, :], hb)
    vt_ref[0] = _dot_nt(wkvt_ref[W_A:2 * W_A, :], hb)
    b_ref[...] = _dot(hb, w_ref[:, 3 * W_A:3 * W_A + 4 * W_R])
    c_ref[...] = _dot(hb, w_ref[:, 3 * W_A + 4 * W_R:D_IN])


def _in_proj_t(x, g, w_bf, b, l):
    m = x.shape[0]
    tm = min(l, 512)
    per_seq = l // tm
    wkvt = w_bf[:, W_A:3 * W_A].T
    row = lambda w: pl.BlockSpec((tm, w), lambda i: (i, 0))
    tr = pl.BlockSpec((1, W_A, tm), lambda i: (i // per_seq, 0, i % per_seq))
    return pl.pallas_call(
        _in_proj_t_kernel,
        grid=(m // tm,),
        in_specs=[row(D_MODEL), _const_spec((1, D_MODEL)), _const_spec((D_MODEL, D_IN)),
                  _const_spec((2 * W_A, D_MODEL))],
        out_specs=[row(W_A), tr, tr, row(4 * W_R), row(C_PROJ)],
        out_shape=[jax.ShapeDtypeStruct((m, W_A), F32),
                   jax.ShapeDtypeStruct((b, W_A, l), F32),
                   jax.ShapeDtypeStruct((b, W_A, l), F32),
                   jax.ShapeDtypeStruct((m, 4 * W_R), F32),
                   jax.ShapeDtypeStruct((m, C_PROJ), F32)],
        compiler_params=_cparams(("parallel",)),
        name="in_proj_t",
    )(x, g.reshape(1, D_MODEL), w_bf, wkvt)


def _matmul_kernel(x_ref, w_ref, o_ref):
    o_ref[...] = _dot(x_ref[...].astype(BF16), w_ref[...])


def _matmul(x, w_bf):
    m, k = x.shape
    n = w_bf.shape[1]
    tm = min(m, 512)
    return pl.pallas_call(
        _matmul_kernel,
        grid=(m // tm,),
        in_specs=[pl.BlockSpec((tm, k), lambda i: (i, 0)), _const_spec((k, n))],
        out_specs=pl.BlockSpec((tm, n), lambda i: (i, 0)),
        out_shape=jax.ShapeDtypeStruct((m, n), F32),
        compiler_params=_cparams(("parallel",)),
        name="matmul",
    )(x, w_bf)


def _out_q_kernel(x_ref, oa_ref, ob_ref, oc_ref, wo_ref, g_ref, wq_ref, x1_ref, qm_ref):
    x1 = (x_ref[...]
          + _dot(oa_ref[...].astype(BF16), wo_ref[0:W_A, :])
          + _dot(ob_ref[...].astype(BF16), wo_ref[W_A:W_A + W_R, :])
          + _dot(oc_ref[...].astype(BF16), wo_ref[W_A + W_R:W_A + 2 * W_R, :]))
    x1_ref[...] = x1
    qm_ref[...] = _dot(_rms(x1, g_ref[...]).astype(BF16), wq_ref[...])


def _out_q(x, oa, ob, oc, wo_bf, g, wq_bf):
    m = x.shape[0]
    tm = min(m, 512)
    row = lambda w: pl.BlockSpec((tm, w), lambda i: (i, 0))
    return pl.pallas_call(
        _out_q_kernel,
        grid=(m // tm,),
        in_specs=[row(D_MODEL), row(W_A), row(W_R), row(W_R),
                  _const_spec((D_MODEL, D_MODEL)), _const_spec((1, D_MODEL)),
                  _const_spec((D_MODEL, D_MODEL))],
        out_specs=[row(D_MODEL), row(D_MODEL)],
        out_shape=[jax.ShapeDtypeStruct((m, D_MODEL), F32)] * 2,
        compiler_params=_cparams(("parallel",)),
        name="out_q",
    )(x, oa, ob, oc, wo_bf, g.reshape(1, D_MODEL), wq_bf)


def _mem_attn_kernel(x_ref, q_ref, mk_ref, mv_ref, wo_ref, o_ref):
    sls = [slice(h * D_M, (h + 1) * D_M) for h in range(H_M)]
    qs = [(q_ref[0, :, sl] * (D_M ** -0.5)).astype(BF16) for sl in sls]
    ss = [_dot_nt(q, mk_ref[0, 0, :, sl].astype(BF16)) for q, sl in zip(qs, sls)]
    es = [jnp.exp(s - jnp.max(s, axis=-1, keepdims=True)) for s in ss]
    ps = [(e / jnp.sum(e, axis=-1, keepdims=True)).astype(BF16) for e in es]
    ohs = [_dot(p, mv_ref[0, 0, :, sl].astype(BF16)).astype(BF16) for p, sl in zip(ps, sls)]
    acc = x_ref[0]
    for oh, sl in zip(ohs, sls):
        acc = acc + _dot(oh, wo_ref[sl, :])
    o_ref[0] = acc


def _mem_attn(x1, qm, mk, mv, group, wo_bf):
    b, l, _ = x1.shape
    tl = min(l, 512)
    tok = pl.BlockSpec((1, tl, D_MODEL), lambda i, j: (i, j, 0))
    mem = pl.BlockSpec((1, 1, N_MEM, D_MODEL), lambda i, j: (group, i, 0, 0))
    return pl.pallas_call(
        _mem_attn_kernel,
        grid=(b, l // tl),
        in_specs=[tok, tok, mem, mem, _const_spec((D_MODEL, D_MODEL))],
        out_specs=tok,
        out_shape=jax.ShapeDtypeStruct((b, l, D_MODEL), F32),
        compiler_params=_cparams(("parallel", "parallel")),
        name="mem_attn",
    )(x1, qm, mk, mv, wo_bf)


FF_CHUNK = 256


def _ffn_kernel(x_ref, g_ref, wg_ref, wu_ref, wd_ref, gf_ref, o_ref, acc_ref, *, final_norm):
    x = x_ref[...]
    hb = _rms(x, g_ref[...]).astype(BF16)
    acc_ref[...] = x
    d_ff = wg_ref.shape[1]
    for c in range(d_ff // FF_CHUNK):
        sl = slice(c * FF_CHUNK, (c + 1) * FF_CHUNK)
        gate = _dot(hb, wg_ref[:, sl])
        up = _dot(hb, wu_ref[:, sl])
        a = (gate * jax.nn.sigmoid(gate)) * up
        acc_ref[...] += _dot(a.astype(BF16), wd_ref[sl, :])
    if final_norm:
        o_ref[...] = _rms(acc_ref[...], gf_ref[...])
    else:
        o_ref[...] = acc_ref[...]


def _ffn(x, g, wg_bf, wu_bf, wd_bf, gf, final_norm):
    m = x.shape[0]
    d_ff = wg_bf.shape[1]
    tm = min(m, 512)
    row = pl.BlockSpec((tm, D_MODEL), lambda i: (i, 0))
    return pl.pallas_call(
        functools.partial(_ffn_kernel, final_norm=final_norm),
        grid=(m // tm,),
        in_specs=[row, _const_spec((1, D_MODEL)), _const_spec((D_MODEL, d_ff)),
                  _const_spec((D_MODEL, d_ff)), _const_spec((d_ff, D_MODEL)),
                  _const_spec((1, D_MODEL))],
        out_specs=row,
        out_shape=jax.ShapeDtypeStruct((m, D_MODEL), F32),
        scratch_shapes=[pltpu.VMEM((tm, D_MODEL), F32)],
        compiler_params=_cparams(("parallel",)),
        name="ffn",
    )(x, g.reshape(1, D_MODEL), wg_bf, wu_bf, wd_bf, gf.reshape(1, D_MODEL))


def _log_terms(z):
    nln = jnp.maximum(z, 0.0) + _log1p_exp(z)
    return nln, z - nln


def _split_trunc(x):
    hi = lax.bitcast_convert_type(lax.bitcast_convert_type(x, jnp.int32) & jnp.int32(-65536), F32)
    return hi.astype(BF16), (x - hi).astype(BF16)


def _sb_prompt_kernel(bias_ref, q_ref, kt_ref, vt_ref, u_ref, o_ref, qs_ref, kb_ref, vs_ref, carry_ref,
                      ls_ref, hl_ref, w_ref):
    i = pl.program_id(1)
    nkb = kt_ref.shape[2] // PAGE
    lane = lax.broadcasted_iota(jnp.int32, (PAGE, LANES), 1)
    row = lax.broadcasted_iota(jnp.int32, (PAGE, LANES), 0)
    even = lane < D_A
    even_row = row < D_A
    causal = lane < row

    @pl.when(i == 0)
    def _():
        for j in range(nkb):
            cols = slice(j * PAGE, (j + 1) * PAGE)
            kb_ref[j] = kt_ref[0, :, cols].astype(BF16)
            for p in range(H_A // 2):
                v2 = vt_ref[0, p * LANES:(p + 1) * LANES, cols]
                vs_ref[p, j, :, 0:PAGE] = jnp.where(even_row, v2, 0.0).astype(BF16)
                vs_ref[p, j, :, PAGE:2 * PAGE] = jnp.where(even_row, 0.0, v2).astype(BF16)

    for p in range(H_A // 2):
        q2 = q_ref[0, :, p * LANES:(p + 1) * LANES] * (D_A ** -0.5)
        qs_ref[p, 0:PAGE, :] = jnp.where(even, q2, 0.0).astype(BF16)
        qs_ref[p, PAGE:2 * PAGE, :] = jnp.where(even, 0.0, q2).astype(BF16)

    def blocks(js, diag):
        for n, j in enumerate(js):
            for p in range(H_A // 2):
                s2 = _dot(qs_ref[p], kb_ref[j, p * LANES:(p + 1) * LANES, :])
                for e in range(2):
                    h = 2 * p + e
                    z = s2[e * PAGE:(e + 1) * PAGE] + bias_ref[h]
                    nln, ls = _log_terms(z)
                    if diag and n == 0:
                        nln = jnp.where(causal, nln, 0.0)
                    hi, lo = _split_trunc(nln)
                    ls_ref[n, h] = ls
                    hl_ref[n, h, :, 0:PAGE] = hi
                    hl_ref[n, h, :, PAGE:2 * PAGE] = lo
        for n, j in enumerate(js):
            for h in range(H_A):
                r = _dot(hl_ref[n, h], u_ref[...])
                between, tot = r[:, 0:LANES], r[:, LANES:2 * LANES]
                if diag and n == 0:
                    w = jnp.where(causal, jnp.exp(ls_ref[n, h] + between), 0.0)
                    carry_ref[h] = tot
                else:
                    c = carry_ref[h]
                    w = jnp.exp(ls_ref[n, h] + between + c)
                    carry_ref[h] = c + tot
                w_ref[n, h // 2, :, (h % 2) * PAGE:(h % 2 + 1) * PAGE] = w.astype(BF16)
        for p in range(H_A // 2):
            pv = _dot_nt(w_ref[0, p], vs_ref[p, js[0]])
            for n in range(1, len(js)):
                pv = pv + _dot_nt(w_ref[n, p], vs_ref[p, js[n]])
            if diag:
                o_ref[0, :, p * LANES:(p + 1) * LANES] = pv
            else:
                o_ref[0, :, p * LANES:(p + 1) * LANES] += pv

    blocks([i], True)
    odd = i % 2

    @pl.when(odd == 1)
    def _():
        blocks([i - 1], False)

    def older(t, _):
        j = i - 1 - odd - 2 * t
        blocks([j, j - 1], False)
        return 0

    lax.fori_loop(0, i // 2, older, 0)


def _sb_tri():
    j = np.arange(2 * PAGE)[:, None] % PAGE
    s = np.arange(2 * LANES)[None, :]
    return jnp.asarray(-((j > s) | (s >= LANES)).astype(np.float32), dtype=BF16)


def _sb_prompt(q, kt, vt, bias):
    b, l, _ = q.shape
    nq = l // PAGE
    tok = pl.BlockSpec((1, PAGE, W_A), lambda i, j: (i, j, 0))
    seq = pl.BlockSpec((1, W_A, l), lambda i, j: (i, 0, 0))
    return pl.pallas_call(
        _sb_prompt_kernel,
        grid=(b, nq),
        in_specs=[pl.BlockSpec(memory_space=pltpu.SMEM), tok, seq, seq,
                  _const_spec((2 * PAGE, 2 * LANES))],
        out_specs=tok,
        out_shape=jax.ShapeDtypeStruct((b, l, W_A), F32),
        scratch_shapes=[pltpu.VMEM((H_A // 2, 2 * PAGE, LANES), BF16),
                        pltpu.VMEM((nq, W_A, PAGE), BF16),
                        pltpu.VMEM((H_A // 2, nq, LANES, 2 * PAGE), BF16),
                        pltpu.VMEM((H_A, PAGE, LANES), F32),
                        pltpu.VMEM((2, H_A, PAGE, LANES), F32),
                        pltpu.VMEM((2, H_A, PAGE, 2 * PAGE), BF16),
                        pltpu.VMEM((2, H_A // 2, PAGE, 2 * PAGE), BF16)],
        compiler_params=_cparams(("parallel", "arbitrary")),
        name="sb_prompt",
    )(bias, q, kt, vt, _sb_tri())


N_DEC = 4


SB_PAGES = 16
N_QROW = N_DEC * H_A


def _sb_sample_kernel(pt_ref, q_ref, br_ref, kn_ref, vn_ref, *rest):
    del pt_ref
    kp_refs, vp_refs = rest[:SB_PAGES], rest[SB_PAGES:2 * SB_PAGES]
    u_ref, o_ref, acc_ref, carry_ref = rest[2 * SB_PAGES:]
    j = pl.program_id(1)

    def logits(kt):
        z = _dot(q_ref[0], kt.astype(BF16)) + br_ref[...]
        return _log_terms(z)

    def suffix(nln):
        hi, lo = _split_trunc(nln)
        r = _dot(jnp.concatenate([hi, lo], axis=1), u_ref[...])
        return r[:, 0:LANES], r[:, LANES:2 * LANES]

    @pl.when(j == 0)
    def _():
        key = lax.broadcasted_iota(jnp.int32, (N_QROW, LANES), 1)
        tok = lax.broadcasted_iota(jnp.int32, (N_QROW, LANES), 0) // H_A
        visible = key < tok
        ln, ls = logits(kn_ref[0])
        ln = jnp.where(visible, ln, 0.0)
        between, tot = suffix(ln)
        w = jnp.where(visible, jnp.exp(ls + between), 0.0)
        carry_ref[...] = tot
        acc_ref[...] = _dot_nt(w.astype(BF16), vn_ref[0].astype(BF16))

    @pl.when(j > 0)
    def _():
        terms = [logits(kp_refs[i][0, 0].reshape(W_A, PAGE)) for i in range(SB_PAGES)]
        sums = [suffix(ln) for ln, _ in terms]
        c = carry_ref[...]
        ws = []
        for (_, ls), (between, tot) in zip(terms, sums):
            ws.append(jnp.exp(ls + between + c).astype(BF16))
            c = c + tot
        carry_ref[...] = c
        acc = acc_ref[...]
        for i in range(SB_PAGES):
            acc = acc + _dot_nt(ws[i], vp_refs[i][0, 0].reshape(W_A, PAGE).astype(BF16))
        acc_ref[...] = acc

    @pl.when(j == pl.num_programs(1) - 1)
    def _():
        head = lax.broadcasted_iota(jnp.int32, (H_A, W_A), 0)
        lane_head = lax.broadcasted_iota(jnp.int32, (H_A, W_A), 1) // D_A
        for t in range(N_DEC):
            rows = acc_ref[t * H_A:(t + 1) * H_A, :]
            o_ref[0, t:t + 1, :] = jnp.sum(jnp.where(head == lane_head, rows, 0.0), axis=0, keepdims=True)


def _sb_sample(q, k, v, bias, cache_kt, cache_vt, layer, page_table):
    b = q.shape[0]
    n_pages = page_table.shape[1]
    eye = jnp.eye(H_A, dtype=F32)
    q4 = q.reshape(b, N_DEC, H_A, D_A) * (D_A ** -0.5)
    qbd = jnp.einsum('bthd,hg->btghd', q4, eye).reshape(b, N_QROW, W_A).astype(BF16)
    bias_rows = jnp.broadcast_to(jnp.tile(bias, N_DEC)[:, None], (N_QROW, LANES))
    new_t = lambda a: jnp.pad(jnp.swapaxes(a, 1, 2), ((0, 0), (0, 0), (0, PAGE - N_DEC)))

    def page_map(slot):
        def index(i, j, pt):
            return (layer, pt[i, n_pages - 1 - ((jnp.maximum(j, 1) - 1) * SB_PAGES + slot)], 0, 0, 0)
        return pl.BlockSpec((1, 1, H_A, D_A, PAGE), index)

    per_b = lambda shape: pl.BlockSpec((1,) + shape, lambda i, j, pt: (i, 0, 0))
    pages = [page_map(s) for s in range(SB_PAGES)]
    grid_spec = pltpu.PrefetchScalarGridSpec(
        num_scalar_prefetch=1,
        grid=(b, n_pages // SB_PAGES + 1),
        in_specs=[per_b((N_QROW, W_A)),
                  pl.BlockSpec((N_QROW, LANES), lambda i, j, pt: (0, 0)),
                  per_b((W_A, PAGE)), per_b((W_A, PAGE))] + pages + pages
                 + [pl.BlockSpec((2 * PAGE, 2 * LANES), lambda i, j, pt: (0, 0))],
        out_specs=per_b((N_DEC, W_A)),
        scratch_shapes=[pltpu.VMEM((N_QROW, W_A), F32), pltpu.VMEM((N_QROW, LANES), F32)],
    )
    return pl.pallas_call(
        _sb_sample_kernel,
        grid_spec=grid_spec,
        out_shape=jax.ShapeDtypeStruct((b, N_DEC, W_A), F32),
        compiler_params=_cparams(("parallel", "arbitrary")),
        name="sb_sample",
    )(page_table, qbd, bias_rows, new_t(k), new_t(v), *([cache_kt] * SB_PAGES), *([cache_vt] * SB_PAGES),
      _sb_tri())


def _mm(a, b, kind="nn", passes=1):
    dot = {"nn": _dot, "nt": _dot_nt, "tn": _dot_tn}[kind]
    if passes == 1:
        return dot(a.astype(BF16), b.astype(BF16))
    a_hi, a_lo = _split2(a)
    if passes == 2:
        b_hi = b.astype(BF16)
        return dot(a_hi, b_hi) + dot(a_lo, b_hi)
    b_hi, b_lo = _split2(b)
    return dot(a_hi, b_hi) + (dot(a_hi, b_lo) + dot(a_lo, b_hi))


def _head_ones(n):
    r = np.arange(n)[:, None] // D_R
    c = np.arange(n)[None, :] // D_R
    return jnp.asarray((r == c).astype(np.float32), dtype=BF16)


def _head_sum(x, ones):
    hi, mid, lo = _split3(x)
    return _dot(hi, ones) + (_dot(mid, ones) + _dot(lo, ones))


def _softplus(x):
    return jnp.maximum(x, 0.0) + _log1p_exp(x)


HG_SUB = 16


def _hgrn_kernel(q_ref, f_ref, i_ref, g_ref, la_ref, l1_ref, gn_ref, s0_ref, bt_ref, ones_ref,
                 o_ref, sf_ref, s_ref, *, valid):
    c = pl.program_id(1)

    @pl.when(c == 0)
    def _():
        s_ref[...] = s0_ref[0]

    tc = q_ref.shape[1]
    bq = q_ref[0]
    qs = bq * jax.nn.sigmoid(bq)
    bf = f_ref[0]
    la = la_ref[...]
    l1p = _log1p_exp(bf)
    lb = l1_ref[...] + (jnp.minimum(bf, 0.0) - l1p)
    lf = jnp.maximum(la, lb) + _log1p_exp(la - lb)
    kk = jnp.exp(l1_ref[...] - (jnp.maximum(bf, 0.0) + l1p))
    if valid < tc:
        live = lax.broadcasted_iota(jnp.int32, (tc, W_R), 0) < valid
        lf = jnp.where(live, lf, 0.0)
        kk = jnp.where(live, kk, 0.0)
    hi, mid, lo = _split3(lf)
    cum = _dot(bt_ref[...], hi) + (_dot(bt_ref[...], mid) + _dot(bt_ref[...], lo))
    iv = i_ref[0]

    trow = lax.broadcasted_iota(jnp.int32, (HG_SUB, LANES), 0)
    r2 = lax.broadcasted_iota(jnp.int32, (LANES, LANES), 0) // D_R
    c2 = lax.broadcasted_iota(jnp.int32, (LANES, LANES), 1) // D_R
    same_head = r2 == c2
    ones_pair = ones_ref[0:LANES, 0:LANES]

    n_sub = tc // HG_SUB
    units = [(sc, p) for sc in range(n_sub) for p in range(2)]
    part = lambda x, sc, p: x[sc * HG_SUB:(sc + 1) * HG_SUB, p * LANES:(p + 1) * LANES]

    scores, upd, q_dec, decay, ivs = {}, {}, {}, {}, {}
    for u in units:
        cumc, qc, kc, ic = (part(x, *u) for x in (cum, qs, kk, iv))
        last = cumc[HG_SUB - 1:HG_SUB]
        ps = []
        for j in range(HG_SUB):
            d = jnp.where(trow >= j, cumc - cumc[j:j + 1], NEG_BIG)
            ps.append(jnp.exp(d) * qc * kc[j:j + 1])
        scores[u] = _dot(jnp.concatenate(ps, axis=0).astype(BF16), ones_pair)
        upd[u] = _dot_tn(ic.astype(BF16), (kc * jnp.exp(last - cumc)).astype(BF16))
        q_dec[u] = (qc * jnp.exp(cumc)).astype(BF16)
        decay[u] = jnp.exp(last)
        ivs[u] = ic

    s = [s_ref[0], s_ref[1]]
    inter = {}
    for u in units:
        p = u[1]
        inter[u] = _dot_nt(q_dec[u], s[p].astype(BF16))
        s[p] = s[p] * decay[u] + jnp.where(same_head, upd[u], 0.0)
    s_ref[0] = s[0]
    s_ref[1] = s[1]

    outs = {}
    for u in units:
        intra = scores[u][0:HG_SUB] * ivs[u][0:1]
        for j in range(1, HG_SUB):
            intra = intra + scores[u][j * HG_SUB:(j + 1) * HG_SUB] * ivs[u][j:j + 1]
        outs[u] = inter[u] + intra
    o = jnp.concatenate([jnp.concatenate([outs[(sc, p)] for p in range(2)], axis=1) for sc in range(n_sub)],
                        axis=0)
    ms = _head_sum(o * o, ones_ref[...]) * (1.0 / D_R)
    bg = g_ref[0]
    o_ref[0] = (o * lax.rsqrt(ms + RMS_EPS)) * gn_ref[...] * (bg * jax.nn.sigmoid(bg))

    @pl.when(c == pl.num_programs(1) - 1)
    def _():
        sf_ref[0] = s_ref[...]


def _hgrn(bslab, log_lb, log1m_lb, gnorm, s0bd, valid):
    b, l, _ = bslab.shape
    tc = min(l, 128)
    col = lambda k: pl.BlockSpec((1, tc, W_R), lambda i, c: (i, c, k))
    par = _const_spec((1, W_R))
    st = pl.BlockSpec((1, 2, LANES, LANES), lambda i, c: (i, 0, 0, 0))
    t = np.arange(tc)[:, None]
    s = np.arange(tc)[None, :]
    blk_tri = jnp.asarray(((t // HG_SUB == s // HG_SUB) & (s <= t)).astype(np.float32), dtype=BF16)
    return pl.pallas_call(
        functools.partial(_hgrn_kernel, valid=valid),
        grid=(b, l // tc),
        in_specs=[col(0), col(1), col(2), col(3), par, par, par, st,
                  _const_spec((tc, tc)), _const_spec((W_R, W_R))],
        out_specs=[pl.BlockSpec((1, tc, W_R), lambda i, c: (i, c, 0)), st],
        out_shape=[jax.ShapeDtypeStruct((b, l, W_R), F32),
                   jax.ShapeDtypeStruct((b, 2, LANES, LANES), F32)],
        scratch_shapes=[pltpu.VMEM((2, LANES, LANES), F32)],
        compiler_params=_cparams(("parallel", "arbitrary")),
        name="hgrn",
    )(bslab, bslab, bslab, bslab, log_lb.reshape(1, W_R), log1m_lb.reshape(1, W_R),
      gnorm.reshape(1, W_R), s0bd, blk_tri, _head_ones(W_R))


def _pair_state_in(s):
    b = s.shape[0]
    st = jnp.swapaxes(s, 2, 3).reshape(b, 2, 2, D_R, D_R)
    return jnp.einsum('bpevk,ef->bpevfk', st, jnp.eye(2, dtype=s.dtype)).reshape(b, 2, LANES, LANES)


def _pair_state_out(sbd):
    b = sbd.shape[0]
    s6 = sbd.reshape(b, 2, 2, D_R, 2, D_R)
    st = jnp.stack([s6[:, :, 0, :, 0, :], s6[:, :, 1, :, 1, :]], axis=2)
    return jnp.swapaxes(st.reshape(b, H_R, D_R, D_R), 2, 3)


RW_CHUNK = 64
RW_N = H_R * RW_CHUNK


RW_SEQS = 4


def _each(f, *lists):
    return [f(*args) for args in zip(*lists)]


def _rwkv_kernel(c_ref, sh0_ref, s0_ref, mu_ref, w0_ref, wup_ref, a0_ref, aup_ref, gup_ref,
                 kk_ref, ka_ref, rk_ref, lnw_ref, lnb_ref, tri_ref, ones_ref,
                 o_ref, sf_ref, prev_ref, s_ref, *, valid, passes):
    c = pl.program_id(1)

    @pl.when(c == 0)
    def _():
        prev_ref[...] = sh0_ref[...]
        s_ref[...] = s0_ref[...]

    seqs = list(range(RW_SEQS))
    cb = [c_ref[n] for n in seqs]
    tok = lax.broadcasted_iota(jnp.int32, (RW_CHUNK, C_PROJ), 0)
    shifted = [jnp.where(tok == 0, prev_ref[n], pltpu.roll(cb[n], 1, axis=0)) for n in seqs]
    for n in seqs:
        prev_ref[n] = cb[n][RW_CHUNK - 1:RW_CHUNK]
    xm = _each(lambda x, sh: x + (sh - x) * mu_ref[...], cb, shifted)
    r = [x[:, 0:W_R] for x in xm]
    k = [x[:, W_R:2 * W_R] for x in xm]
    v = [x[:, 2 * W_R:3 * W_R] for x in xm]
    tail = [x[:, 3 * W_R:C_PROJ] for x in xm]
    ones = ones_ref[...]

    w_log = _each(lambda t: -_softplus(-(w0_ref[...] + _mm(jnp.tanh(t), wup_ref[...]))) - 0.5, tail)
    lw = [-jnp.exp(x) for x in w_log]
    a = _each(lambda t: jax.nn.sigmoid(a0_ref[...] + _mm(t, aup_ref[...])), tail)
    g = _each(lambda t: _mm(jax.nn.sigmoid(t), gup_ref[...]), tail)
    kk = [x * kk_ref[...] for x in k]
    kk = _each(lambda x: x / jnp.maximum(jnp.sqrt(_head_sum(x * x, ones)), L2_EPS), kk)
    k = _each(lambda x, aa: x * (1.0 + (aa - 1.0) * ka_ref[...]), k, a)
    av = [-x for x in kk]
    bv = _each(lambda x, aa: x * aa, kk, a)
    if valid < RW_CHUNK:
        live = lax.broadcasted_iota(jnp.int32, (RW_CHUNK, W_R), 0) < valid
        zero = lambda xs: [jnp.where(live, x, 0.0) for x in xs]
        lw, av, bv, k, v = zero(lw), zero(av), zero(bv), zero(k), zero(v)

    def cumsum(x):
        hi, mid, lo = _split3(x)
        return _dot(tri_ref[...], hi) + (_dot(tri_ref[...], mid) + _dot(tri_ref[...], lo))

    cl = _each(cumsum, lw)
    last = [x[RW_CHUNK - 1:RW_CHUNK] for x in cl]
    e_neg = [jnp.exp(-x) for x in cl]
    e_end = _each(lambda l, x: jnp.exp(l - x), last, cl)

    lane_head = lax.broadcasted_iota(jnp.int32, (RW_CHUNK, W_R), 1) // D_R

    def stack(x):
        return jnp.concatenate([jnp.where(lane_head == h, x, 0.0) for h in range(H_R)], axis=0)

    m_a = _each(lambda x, c_, l_: stack(x * jnp.exp(c_ - l_)), av, cl, lw)
    m_b = _each(lambda x, e: stack(x * e), bv, e_neg)
    m_k = _each(lambda x, e: stack(x * e), k, e_neg)
    m_r = _each(lambda x, c_: stack(x * jnp.exp(c_)), r, cl)
    m_v = _each(stack, v)
    m_bh = _each(lambda x, e: stack(x * e), bv, e_end)
    m_kh = _each(lambda x, e: stack(x * e), k, e_end)

    row = lax.broadcasted_iota(jnp.int32, (RW_N, RW_N), 0)
    col = lax.broadcasted_iota(jnp.int32, (RW_N, RW_N), 1)
    mm = functools.partial(_mm, passes=passes)
    nt = functools.partial(mm, kind="nt")
    tn = functools.partial(mm, kind="tn")
    add = lambda x, y: x + y
    t_ab = _each(lambda x, y: jnp.where(row > col, nt(x, y), 0.0), m_a, m_b)
    t_ak = _each(lambda x, y: jnp.where(row > col, nt(x, y), 0.0), m_a, m_k)
    t_rb = _each(lambda x, y: jnp.where(row >= col, nt(x, y), 0.0), m_r, m_b)
    t_rk = _each(lambda x, y: jnp.where(row >= col, nt(x, y), 0.0), m_r, m_k)

    inv = [jnp.where(row == col, 1.0, t) for t in t_ab]
    pw = t_ab
    for _ in range(5):
        pw = _each(mm, pw, pw)
        inv = _each(add, inv, _each(mm, inv, pw))

    w1 = _each(mm, inv, m_a)
    w2 = _each(mm, inv, _each(mm, t_ak, m_v))
    rm = _each(add, m_r, _each(mm, t_rb, w1))
    yc = _each(add, _each(mm, t_rb, w2), _each(mm, t_rk, m_v))
    pm = _each(lambda l, x, y: jnp.where(row == col, jnp.exp(l), 0.0) + tn(x, y), last, w1, m_bh)
    qm = _each(add, _each(tn, w2, m_bh), _each(tn, m_v, m_kh))

    s = [s_ref[n] for n in seqs]
    y_st = _each(add, _each(nt, rm, s), yc)
    s_new = _each(lambda x, p_, q_: _mm(x, p_, passes=max(passes, 2)) + q_, s, pm, qm)
    for n in seqs:
        s_ref[n] = s_new[n]
    y = [(t[0:RW_CHUNK] + t[RW_CHUNK:2 * RW_CHUNK]) + (t[2 * RW_CHUNK:3 * RW_CHUNK] + t[3 * RW_CHUNK:])
         for t in y_st]

    mean = [_head_sum(x, ones) * (1.0 / D_R) for x in y]
    yc0 = _each(lambda x, m: x - m, y, mean)
    var = [_head_sum(x * x, ones) * (1.0 / D_R) for x in yc0]
    yn = _each(lambda x, vv: x * lax.rsqrt(vv + GN_EPS) * lnw_ref[...] + lnb_ref[...], yc0, var)
    bonus = _each(lambda rr, kk_, vv: _head_sum(rr * kk_ * rk_ref[...], ones) * vv, r, k, v)
    for n in seqs:
        o_ref[n] = (yn[n] + bonus[n]) * g[n]

    @pl.when(c == pl.num_programs(1) - 1)
    def _():
        sf_ref[...] = s_ref[...]


def _rwkv(cslab, shift0, s0bd, p, valid, passes=1):
    b, l, _ = cslab.shape
    row = lambda a: a.reshape(1, -1)
    pad_up = lambda w, off: jnp.zeros((LANES, W_R), F32).at[off:off + w.shape[0]].set(w).astype(BF16)
    t = np.arange(RW_CHUNK)[:, None]
    s = np.arange(RW_CHUNK)[None, :]
    consts = [row(p['rwkv_mu']), row(p['rwkv_w0']), pad_up(p['rwkv_w_up'], 0), row(p['rwkv_a0']),
              pad_up(p['rwkv_a_up'], 32), pad_up(p['rwkv_g_up'], 64), row(p['rwkv_k_k']),
              row(p['rwkv_k_a']), row(p['rwkv_r_k']), row(p['rwkv_ln_w']), row(p['rwkv_ln_b']),
              jnp.asarray((s <= t).astype(np.float32), dtype=BF16), _head_ones(W_R)]
    assert b % RW_SEQS == 0
    st = pl.BlockSpec((RW_SEQS, RW_N, RW_N), lambda i, c: (i, 0, 0))
    return pl.pallas_call(
        functools.partial(_rwkv_kernel, valid=valid, passes=passes),
        grid=(b // RW_SEQS, l // RW_CHUNK),
        in_specs=[pl.BlockSpec((RW_SEQS, RW_CHUNK, C_PROJ), lambda i, c: (i, c, 0)),
                  pl.BlockSpec((RW_SEQS, 1, C_PROJ), lambda i, c: (i, 0, 0)), st]
                 + [_const_spec(a.shape) for a in consts],
        out_specs=[pl.BlockSpec((RW_SEQS, RW_CHUNK, W_R), lambda i, c: (i, c, 0)), st],
        out_shape=[jax.ShapeDtypeStruct((b, l, W_R), F32),
                   jax.ShapeDtypeStruct((b, RW_N, RW_N), F32)],
        scratch_shapes=[pltpu.VMEM((RW_SEQS, 1, C_PROJ), F32), pltpu.VMEM((RW_SEQS, RW_N, RW_N), F32)],
        compiler_params=_cparams(("parallel", "arbitrary")),
        name="rwkv",
    )(cslab, shift0, s0bd, *consts)


def _quad_state_in(s):
    b = s.shape[0]
    return jnp.einsum('bhij,hg->bhigj', s, jnp.eye(H_R, dtype=s.dtype)).reshape(b, RW_N, RW_N)


def _quad_state_out(sbd):
    b = sbd.shape[0]
    s5 = sbd.reshape(b, H_R, D_R, H_R, D_R)
    return jnp.stack([s5[:, h, :, h, :] for h in range(H_R)], axis=1)


def _pad_tokens(a, multiple):
    pad = (-a.shape[1]) % multiple
    return a if pad == 0 else jnp.pad(a, ((0, 0), (0, pad), (0, 0)))


def _trunk_layer(x, b, l, sb_fn, s_hgrn0, s_rwkv0, shift0, mem_kv, lb, p, final_gain, prompt):
    if prompt:
        q, kt, vt, bslab, cslab = _in_proj_t(x, p['norm_mix'], p['w_in'], b, l)
        o_a = sb_fn(q.reshape(b, l, W_A), kt, vt)
        heads_last = lambda a: jnp.transpose(a.reshape(b, H_A, D_A, l), (0, 3, 1, 2))
        k, v = heads_last(kt), heads_last(vt)
    else:
        q, k, v, bslab, cslab = _in_proj(x, p['norm_mix'], p['w_in'])
        o_a = sb_fn(q.reshape(b, l, W_A), k.reshape(b, l, W_A), v.reshape(b, l, W_A))
    o_b, s_hgrn = _hgrn(_pad_tokens(bslab.reshape(b, l, 4 * W_R), HG_SUB), jnp.log(lb), jnp.log1p(-lb),
                        p['hgrn_norm'], _pair_state_in(s_hgrn0), l)
    cslab = cslab.reshape(b, l, C_PROJ)
    o_c, s_rwkv = _rwkv(_pad_tokens(cslab, RW_CHUNK), shift0.reshape(b, 1, C_PROJ),
                        _quad_state_in(s_rwkv0), p, l)
    x1, qm = _out_q(x, o_a.reshape(b * l, W_A), o_b[:, :l].reshape(b * l, W_R),
                    o_c[:, :l].reshape(b * l, W_R), p['w_out'], p['norm_mem'], p['mem_wq'])
    x2 = _mem_attn(x1.reshape(b, l, D_MODEL), qm.reshape(b, l, D_MODEL), *mem_kv, p['mem_wo'])
    x3 = _ffn(x2.reshape(b * l, D_MODEL), p['norm_ffn'], p['ffn_w_gate'], p['ffn_w_up'], p['ffn_w_down'],
              p['norm_ffn'] if final_gain is None else final_gain, final_gain is not None)
    return (x3, k.reshape(b, l, H_A, D_A), v.reshape(b, l, H_A, D_A), _pair_state_out(s_hgrn),
            _quad_state_out(s_rwkv), cslab[:, l - 1])


def kernel(x_prompt, x_sample, mem_prompt, cache_sb_k, cache_sb_v, page_table, state_hgrn, state_rwkv,
           state_rwkv_shift, cache_mem_k, cache_mem_v, norm_mix, w_in, sb_bias, hgrn_lb, hgrn_norm, rwkv_mu,
           rwkv_w0, rwkv_w_up, rwkv_a0, rwkv_a_up, rwkv_g_up, rwkv_k_k, rwkv_k_a, rwkv_r_k, rwkv_ln_w, rwkv_ln_b,
           w_out, norm_mem, mem_wq, mem_wk, mem_wv, mem_wo, norm_ffn, ffn_w_gate, ffn_w_up, ffn_w_down,
           final_norm):
    bp, lp, _ = x_prompt.shape
    bs, ls, _ = x_sample.shape
    depth = w_in.shape[0]
    n_pool = cache_sb_k.shape[1]
    lb_all = jnp.cumsum(jax.nn.softmax(hgrn_lb.astype(F32), axis=0), axis=0)
    lb_all = lb_all - lb_all[0]
    assert page_table.shape[1] % SB_PAGES == 0 and x_sample.shape[1] == N_DEC
    cache_kt = jnp.transpose(cache_sb_k, (0, 1, 3, 4, 2))
    cache_vt = jnp.transpose(cache_sb_v, (0, 1, 3, 4, 2))
    mem_tok = mem_prompt.reshape(bp * N_MEM, D_MODEL)
    mem_k_s = cache_mem_k.reshape(depth, bs, N_MEM, D_MODEL)
    mem_v_s = cache_mem_v.reshape(depth, bs, N_MEM, D_MODEL)
    xp = x_prompt.reshape(bp * lp, D_MODEL)
    xs = x_sample.reshape(bs * ls, D_MODEL)
    outs = [[] for _ in range(12)]
    for l in range(depth):
        bf = lambda w: w[l].astype(BF16)
        p = dict(norm_mix=norm_mix[l], w_in=bf(w_in), hgrn_norm=hgrn_norm[l], rwkv_mu=rwkv_mu[l],
                 rwkv_w0=rwkv_w0[l], rwkv_w_up=rwkv_w_up[l], rwkv_a0=rwkv_a0[l], rwkv_a_up=rwkv_a_up[l],
                 rwkv_g_up=rwkv_g_up[l], rwkv_k_k=rwkv_k_k[l], rwkv_k_a=rwkv_k_a[l], rwkv_r_k=rwkv_r_k[l],
                 rwkv_ln_w=rwkv_ln_w[l], rwkv_ln_b=rwkv_ln_b[l], w_out=bf(w_out), norm_mem=norm_mem[l],
                 mem_wq=bf(mem_wq), mem_wo=bf(mem_wo), norm_ffn=norm_ffn[l], ffn_w_gate=bf(ffn_w_gate),
                 ffn_w_up=bf(ffn_w_up), ffn_w_down=bf(ffn_w_down))
        gain = final_norm if l == depth - 1 else None
        bias = sb_bias[l]
        mk_p = _matmul(mem_tok, bf(mem_wk)).reshape(bp, N_MEM, D_MODEL)
        mv_p = _matmul(mem_tok, bf(mem_wv)).reshape(bp, N_MEM, D_MODEL)
        xp, kp, vp, hg_p, rw_p, sh_p = _trunk_layer(
            xp, bp, lp, functools.partial(_sb_prompt, bias=bias),
            jnp.zeros((bp, H_R, D_R, D_R), F32), jnp.zeros((bp, H_R, D_R, D_R), F32),
            jnp.zeros((bp, C_PROJ), F32), (mk_p[None], mv_p[None], 0), lb_all[l], p, gain, True)
        sb_fn = functools.partial(_sb_sample, bias=bias, cache_kt=cache_kt, cache_vt=cache_vt, layer=l,
                                  page_table=page_table)
        xs, ks, vs, hg_s, rw_s, sh_s = _trunk_layer(
            xs, bs, ls, sb_fn, state_hgrn[l], state_rwkv[l], state_rwkv_shift[l],
            (mem_k_s, mem_v_s, l), lb_all[l], p, gain, False)
        for lst, val in zip(outs, (kp, vp, ks, vs, hg_p, hg_s, rw_p, rw_s, sh_p, sh_s,
                                   mk_p.reshape(bp, N_MEM, H_M, D_M), mv_p.reshape(bp, N_MEM, H_M, D_M))):
            lst.append(val)
    return (xp.reshape(bp, lp, D_MODEL), xs.reshape(bs, ls, D_MODEL)) + tuple(jnp.stack(o) for o in outs)
```

```python
import functools

import jax
import jax.numpy as jnp
from jax import lax
from jax.experimental import pallas as pl
from jax.experimental.pallas import tpu as pltpu
import numpy as np

F32 = jnp.float32
BF16 = jnp.bfloat16

D_MODEL = 1024
H_A, D_A = 8, 64
W_A = H_A * D_A
H_R, D_R = 4, 64
W_R = H_R * D_R
H_M, D_M = 4, 256
N_MEM = 256
C_PROJ = 3 * W_R + 128
D_IN = 3 * W_A + 4 * W_R + C_PROJ
PAGE = 128
RMS_EPS = 1e-6
GN_EPS = 64e-5
L2_EPS = 1e-12

LANES = 128
VMEM_LIMIT = 56 * 1024 * 1024

NEG_BIG = -1e30


def _cparams(sem):
    return pltpu.CompilerParams(dimension_semantics=sem, vmem_limit_bytes=VMEM_LIMIT)


def _dot(a, b):
    return jnp.dot(a, b, preferred_element_type=F32)


def _dot_nt(a, b):
    return lax.dot_general(a, b, (((1,), (1,)), ((), ())), preferred_element_type=F32)


def _dot_tn(a, b):
    return lax.dot_general(a, b, (((0,), (0,)), ((), ())), preferred_element_type=F32)


def _split2(x):
    hi = x.astype(BF16)
    lo = (x - hi.astype(F32)).astype(BF16)
    return hi, lo


def _split3(x):
    hi = x.astype(BF16)
    r = x - hi.astype(F32)
    mid = r.astype(BF16)
    lo = (r - mid.astype(F32)).astype(BF16)
    return hi, mid, lo


def _rms(x, g):
    ms = jnp.mean(x * x, axis=-1, keepdims=True)
    return (x * lax.rsqrt(ms + RMS_EPS)) * g


def _log1p_exp(x):
    return jnp.log(1.0 + jnp.exp(-jnp.abs(x)))


def _const_spec(shape):
    nd = len(shape)
    return pl.BlockSpec(shape, lambda *_: (0,) * nd)


def _in_proj_kernel(x_ref, g_ref, w_ref, q_ref, k_ref, v_ref, b_ref, c_ref):
    hb = _rms(x_ref[...], g_ref[...]).astype(BF16)
    q_ref[...] = _dot(hb, w_ref[:, 0:W_A])
    k_ref[...] = _dot(hb, w_ref[:, W_A:2 * W_A])
    v_ref[...] = _dot(hb, w_ref[:, 2 * W_A:3 * W_A])
    b_ref[...] = _dot(hb, w_ref[:, 3 * W_A:3 * W_A + 4 * W_R])
    c_ref[...] = _dot(hb, w_ref[:, 3 * W_A + 4 * W_R:D_IN])


def _in_proj(x, g, w_bf):
    m = x.shape[0]
    tm = min(m, 512)
    widths = (W_A, W_A, W_A, 4 * W_R, C_PROJ)
    return pl.pallas_call(
        _in_proj_kernel,
        grid=(m // tm,),
        in_specs=[pl.BlockSpec((tm, D_MODEL), lambda i: (i, 0)),
                  _const_spec((1, D_MODEL)),
                  _const_spec((D_MODEL, D_IN))],
        out_specs=[pl.BlockSpec((tm, w), lambda i: (i, 0)) for w in widths],
        out_shape=[jax.ShapeDtypeStruct((m, w), F32) for w in widths],
        compiler_params=_cparams(("parallel",)),
        name="in_proj",
    )(x, g.reshape(1, D_MODEL), w_bf)


def _in_proj_t_kernel(x_ref, g_ref, w_ref, wkvt_ref, *rest):
    q_ref, kt_ref, vt_ref, b_ref, c_ref = rest[-5:]
    hb = _rms(x_ref[...], g_ref[...]).astype(BF16)
    q_ref[...] = _dot(hb, w_ref[:, 0:W_A])
    kt_ref[0, 0] = _dot_nt(wkvt_ref[0:W_A, :], hb)
    vt_ref[0, 0] = _dot_nt(wkvt_ref[W_A:2 * W_A, :], hb)
    b_ref[...] = _dot(hb, w_ref[:, 3 * W_A:3 * W_A + 4 * W_R])
    c_ref[...] = _dot(hb, w_ref[:, 3 * W_A + 4 * W_R:D_IN])


def _in_proj_t(x, g, w_bf, b, l, layer, depth, kv_all):
    m = x.shape[0]
    tm = min(l, 512)
    assert l % tm == 0
    per_seq = l // tm
    wkvt = w_bf[:, W_A:3 * W_A].T
    row = lambda w: pl.BlockSpec((tm, w), lambda i: (i, 0))
    tr = pl.BlockSpec((1, 1, W_A, tm), lambda i: (layer, i // per_seq, 0, i % per_seq))
    carried = [] if kv_all is None else list(kv_all)
    return pl.pallas_call(
        _in_proj_t_kernel,
        grid=(m // tm,),
        in_specs=[row(D_MODEL), _const_spec((1, D_MODEL)), _const_spec((D_MODEL, D_IN)),
                  _const_spec((2 * W_A, D_MODEL))] + [pl.BlockSpec(memory_space=pl.ANY)] * len(carried),
        out_specs=[row(W_A), tr, tr, row(4 * W_R), row(C_PROJ)],
        out_shape=[jax.ShapeDtypeStruct((m, W_A), F32),
                   jax.ShapeDtypeStruct((depth, b, W_A, l), F32),
                   jax.ShapeDtypeStruct((depth, b, W_A, l), F32),
                   jax.ShapeDtypeStruct((m, 4 * W_R), F32),
                   jax.ShapeDtypeStruct((m, C_PROJ), F32)],
        input_output_aliases={4: 1, 5: 2} if carried else {},
        compiler_params=_cparams(("parallel",)),
        name="in_proj_t",
    )(x, g.reshape(1, D_MODEL), w_bf, wkvt, *carried)


def _matmul_kernel(x_ref, w_ref, o_ref):
    o_ref[...] = _dot(x_ref[...].astype(BF16), w_ref[...])


def _matmul(x, w_bf):
    m, k = x.shape
    n = w_bf.shape[1]
    tm = min(m, 512)
    return pl.pallas_call(
        _matmul_kernel,
        grid=(m // tm,),
        in_specs=[pl.BlockSpec((tm, k), lambda i: (i, 0)), _const_spec((k, n))],
        out_specs=pl.BlockSpec((tm, n), lambda i: (i, 0)),
        out_shape=jax.ShapeDtypeStruct((m, n), F32),
        compiler_params=_cparams(("parallel",)),
        name="matmul",
    )(x, w_bf)


def _out_q_kernel(x_ref, oa_ref, ob_ref, oc_ref, wo_ref, g_ref, wq_ref, x1_ref, qm_ref):
    x1 = (x_ref[...]
          + _dot(oa_ref[...].astype(BF16), wo_ref[0:W_A, :])
          + _dot(ob_ref[...].astype(BF16), wo_ref[W_A:W_A + W_R, :])
          + _dot(oc_ref[...].astype(BF16), wo_ref[W_A + W_R:W_A + 2 * W_R, :]))
    x1_ref[...] = x1
    qm_ref[...] = _dot(_rms(x1, g_ref[...]).astype(BF16), wq_ref[...])


def _out_q(x, oa, ob, oc, wo_bf, g, wq_bf):
    m = x.shape[0]
    tm = min(m, 512)
    row = lambda w: pl.BlockSpec((tm, w), lambda i: (i, 0))
    return pl.pallas_call(
        _out_q_kernel,
        grid=(m // tm,),
        in_specs=[row(D_MODEL), row(W_A), row(W_R), row(W_R),
                  _const_spec((D_MODEL, D_MODEL)), _const_spec((1, D_MODEL)),
                  _const_spec((D_MODEL, D_MODEL))],
        out_specs=[row(D_MODEL), row(D_MODEL)],
        out_shape=[jax.ShapeDtypeStruct((m, D_MODEL), F32)] * 2,
        compiler_params=_cparams(("parallel",)),
        name="out_q",
    )(x, oa, ob, oc, wo_bf, g.reshape(1, D_MODEL), wq_bf)


def _mem_attn_kernel(x_ref, q_ref, mk_ref, mv_ref, wo_ref, o_ref):
    sls = [slice(h * D_M, (h + 1) * D_M) for h in range(H_M)]
    qs = [(q_ref[0, :, sl] * (D_M ** -0.5)).astype(BF16) for sl in sls]
    ss = [_dot_nt(q, mk_ref[0, 0, :, sl].astype(BF16)) for q, sl in zip(qs, sls)]
    es = [jnp.exp(s - jnp.max(s, axis=-1, keepdims=True)) for s in ss]
    ps = [(e / jnp.sum(e, axis=-1, keepdims=True)).astype(BF16) for e in es]
    ohs = [_dot(p, mv_ref[0, 0, :, sl].astype(BF16)).astype(BF16) for p, sl in zip(ps, sls)]
    acc = x_ref[0]
    for oh, sl in zip(ohs, sls):
        acc = acc + _dot(oh, wo_ref[sl, :])
    o_ref[0] = acc


def _mem_attn(x1, qm, mk, mv, group, wo_bf):
    b, l, _ = x1.shape
    tl = min(l, 512)
    tok = pl.BlockSpec((1, tl, D_MODEL), lambda i, j: (i, j, 0))
    mem = pl.BlockSpec((1, 1, N_MEM, D_MODEL), lambda i, j: (group, i, 0, 0))
    return pl.pallas_call(
        _mem_attn_kernel,
        grid=(b, l // tl),
        in_specs=[tok, tok, mem, mem, _const_spec((D_MODEL, D_MODEL))],
        out_specs=tok,
        out_shape=jax.ShapeDtypeStruct((b, l, D_MODEL), F32),
        compiler_params=_cparams(("parallel", "parallel")),
        name="mem_attn",
    )(x1, qm, mk, mv, wo_bf)


FF_CHUNK = 256


def _ffn_kernel(x_ref, g_ref, wg_ref, wu_ref, wd_ref, gf_ref, o_ref, acc_ref, *, final_norm):
    x = x_ref[...]
    hb = _rms(x, g_ref[...]).astype(BF16)
    acc_ref[...] = x
    d_ff = wg_ref.shape[1]
    for c in range(d_ff // FF_CHUNK):
        sl = slice(c * FF_CHUNK, (c + 1) * FF_CHUNK)
        gate = _dot(hb, wg_ref[:, sl])
        up = _dot(hb, wu_ref[:, sl])
        a = (gate * jax.nn.sigmoid(gate)) * up
        acc_ref[...] += _dot(a.astype(BF16), wd_ref[sl, :])
    if final_norm:
        o_ref[...] = _rms(acc_ref[...], gf_ref[...])
    else:
        o_ref[...] = acc_ref[...]


def _ffn(x, g, wg_bf, wu_bf, wd_bf, gf, final_norm):
    m = x.shape[0]
    d_ff = wg_bf.shape[1]
    tm = min(m, 512)
    row = pl.BlockSpec((tm, D_MODEL), lambda i: (i, 0))
    return pl.pallas_call(
        functools.partial(_ffn_kernel, final_norm=final_norm),
        grid=(m // tm,),
        in_specs=[row, _const_spec((1, D_MODEL)), _const_spec((D_MODEL, d_ff)),
                  _const_spec((D_MODEL, d_ff)), _const_spec((d_ff, D_MODEL)),
                  _const_spec((1, D_MODEL))],
        out_specs=row,
        out_shape=jax.ShapeDtypeStruct((m, D_MODEL), F32),
        scratch_shapes=[pltpu.VMEM((tm, D_MODEL), F32)],
        compiler_params=_cparams(("parallel",)),
        name="ffn",
    )(x, g.reshape(1, D_MODEL), wg_bf, wu_bf, wd_bf, gf.reshape(1, D_MODEL))


def _log_terms(z):
    nln = jnp.maximum(z, 0.0) + _log1p_exp(z)
    return nln, z - nln


def _split_trunc(x):
    hi = lax.bitcast_convert_type(lax.bitcast_convert_type(x, jnp.int32) & jnp.int32(-65536), F32)
    return hi.astype(BF16), (x - hi).astype(BF16)


SB_GROUP = 4


def _sb_prompt_kernel(bias_ref, q_ref, kt_ref, vt_ref, u_ref, o_ref, qs_ref, kb_ref, vs_ref, carry_ref,
                      ls_ref, hl_ref, w_ref):
    i = pl.program_id(1)
    nkb = kt_ref.shape[3] // PAGE
    lane = lax.broadcasted_iota(jnp.int32, (PAGE, LANES), 1)
    row = lax.broadcasted_iota(jnp.int32, (PAGE, LANES), 0)
    even = lane < D_A
    even_row = row < D_A
    causal = lane < row

    @pl.when(i == 0)
    def _():
        for j in range(nkb):
            cols = slice(j * PAGE, (j + 1) * PAGE)
            kb_ref[j] = kt_ref[0, 0, :, cols].astype(BF16)
            for p in range(H_A // 2):
                v2 = vt_ref[0, 0, p * LANES:(p + 1) * LANES, cols]
                vs_ref[p, j, :, 0:PAGE] = jnp.where(even_row, v2, 0.0).astype(BF16)
                vs_ref[p, j, :, PAGE:2 * PAGE] = jnp.where(even_row, 0.0, v2).astype(BF16)

    for p in range(H_A // 2):
        q2 = q_ref[0, :, p * LANES:(p + 1) * LANES] * (D_A ** -0.5)
        qs_ref[p, 0:PAGE, :] = jnp.where(even, q2, 0.0).astype(BF16)
        qs_ref[p, PAGE:2 * PAGE, :] = jnp.where(even, 0.0, q2).astype(BF16)

    def blocks(js, diag):
        for n, j in enumerate(js):
            for p in range(H_A // 2):
                s2 = _dot(qs_ref[p], kb_ref[j, p * LANES:(p + 1) * LANES, :])
                for e in range(2):
                    h = 2 * p + e
                    z = s2[e * PAGE:(e + 1) * PAGE] + bias_ref[h]
                    nln, ls = _log_terms(z)
                    if diag and n == 0:
                        nln = jnp.where(causal, nln, 0.0)
                    hi, lo = _split_trunc(nln)
                    ls_ref[n, h] = ls
                    hl_ref[n, h, :, 0:PAGE] = hi
                    hl_ref[n, h, :, PAGE:2 * PAGE] = lo
        for n, j in enumerate(js):
            for h in range(H_A):
                r = _dot(hl_ref[n, h], u_ref[...])
                between, tot = r[:, 0:LANES], r[:, LANES:2 * LANES]
                if diag and n == 0:
                    w = jnp.where(causal, jnp.exp(ls_ref[n, h] + between), 0.0)
                    carry_ref[h] = tot
                else:
                    c = carry_ref[h]
                    w = jnp.exp(ls_ref[n, h] + between + c)
                    carry_ref[h] = c + tot
                w_ref[n, h // 2, :, (h % 2) * PAGE:(h % 2 + 1) * PAGE] = w.astype(BF16)
        for p in range(H_A // 2):
            pv = _dot_nt(w_ref[0, p], vs_ref[p, js[0]])
            for n in range(1, len(js)):
                pv = pv + _dot_nt(w_ref[n, p], vs_ref[p, js[n]])
            if diag:
                o_ref[0, :, p * LANES:(p + 1) * LANES] = pv
            else:
                o_ref[0, :, p * LANES:(p + 1) * LANES] += pv

    blocks([i], True)
    rem = i % SB_GROUP
    for r in range(1, SB_GROUP):
        @pl.when(rem == r)
        def _(r=r):
            blocks([i - 1 - n for n in range(r)], False)

    def older(t, _):
        j = i - 1 - rem - SB_GROUP * t
        blocks([j - n for n in range(SB_GROUP)], False)
        return 0

    lax.fori_loop(0, i // SB_GROUP, older, 0)


def _sb_tri():
    j = np.arange(2 * PAGE)[:, None] % PAGE
    s = np.arange(2 * LANES)[None, :]
    return jnp.asarray(-((j > s) | (s >= LANES)).astype(np.float32), dtype=BF16)


def _sb_prompt(q, kt, vt, bias, layer=0):
    b, l, _ = q.shape
    nq = l // PAGE
    tok = pl.BlockSpec((1, PAGE, W_A), lambda i, j: (i, j, 0))
    seq = pl.BlockSpec((1, 1, W_A, l), lambda i, j: (layer, i, 0, 0))
    return pl.pallas_call(
        _sb_prompt_kernel,
        grid=(b, nq),
        in_specs=[pl.BlockSpec(memory_space=pltpu.SMEM), tok, seq, seq,
                  _const_spec((2 * PAGE, 2 * LANES))],
        out_specs=tok,
        out_shape=jax.ShapeDtypeStruct((b, l, W_A), F32),
        scratch_shapes=[pltpu.VMEM((H_A // 2, 2 * PAGE, LANES), BF16),
                        pltpu.VMEM((nq, W_A, PAGE), BF16),
                        pltpu.VMEM((H_A // 2, nq, LANES, 2 * PAGE), BF16),
                        pltpu.VMEM((H_A, PAGE, LANES), F32),
                        pltpu.VMEM((SB_GROUP, H_A, PAGE, LANES), F32),
                        pltpu.VMEM((SB_GROUP, H_A, PAGE, 2 * PAGE), BF16),
                        pltpu.VMEM((SB_GROUP, H_A // 2, PAGE, 2 * PAGE), BF16)],
        compiler_params=_cparams(("parallel", "arbitrary")),
        name="sb_prompt",
    )(bias, q, kt, vt, _sb_tri())


N_DEC = 4


SB_PAGES = 16
N_QROW = N_DEC * H_A


def _sb_sample_kernel(pt_ref, q_ref, br_ref, kn_ref, vn_ref, *rest):
    del pt_ref
    kp_refs, vp_refs = rest[:SB_PAGES], rest[SB_PAGES:2 * SB_PAGES]
    u_ref, o_ref, acc_ref, carry_ref = rest[2 * SB_PAGES:]
    j = pl.program_id(1)

    def logits(kt):
        z = _dot(q_ref[0], kt.astype(BF16)) + br_ref[...]
        return _log_terms(z)

    def suffix(nln):
        hi, lo = _split_trunc(nln)
        r = _dot(jnp.concatenate([hi, lo], axis=1), u_ref[...])
        return r[:, 0:LANES], r[:, LANES:2 * LANES]

    @pl.when(j == 0)
    def _():
        key = lax.broadcasted_iota(jnp.int32, (N_QROW, LANES), 1)
        tok = lax.broadcasted_iota(jnp.int32, (N_QROW, LANES), 0) // H_A
        visible = key < tok
        ln, ls = logits(kn_ref[0])
        ln = jnp.where(visible, ln, 0.0)
        between, tot = suffix(ln)
        w = jnp.where(visible, jnp.exp(ls + between), 0.0)
        carry_ref[...] = tot
        acc_ref[...] = _dot_nt(w.astype(BF16), vn_ref[0].astype(BF16))

    @pl.when(j > 0)
    def _():
        terms = [logits(kp_refs[i][0, 0].reshape(W_A, PAGE)) for i in range(SB_PAGES)]
        sums = [suffix(ln) for ln, _ in terms]
        c = carry_ref[...]
        ws = []
        for (_, ls), (between, tot) in zip(terms, sums):
            ws.append(jnp.exp(ls + between + c).astype(BF16))
            c = c + tot
        carry_ref[...] = c
        acc = acc_ref[...]
        for i in range(SB_PAGES):
            acc = acc + _dot_nt(ws[i], vp_refs[i][0, 0].reshape(W_A, PAGE).astype(BF16))
        acc_ref[...] = acc

    @pl.when(j == pl.num_programs(1) - 1)
    def _():
        head = lax.broadcasted_iota(jnp.int32, (H_A, W_A), 0)
        lane_head = lax.broadcasted_iota(jnp.int32, (H_A, W_A), 1) // D_A
        for t in range(N_DEC):
            rows = acc_ref[t * H_A:(t + 1) * H_A, :]
            o_ref[0, t:t + 1, :] = jnp.sum(jnp.where(head == lane_head, rows, 0.0), axis=0, keepdims=True)


def _sb_sample(q, k, v, bias, cache_kt, cache_vt, layer, page_table):
    b = q.shape[0]
    n_pages = page_table.shape[1]
    eye = jnp.eye(H_A, dtype=F32)
    q4 = q.reshape(b, N_DEC, H_A, D_A) * (D_A ** -0.5)
    qbd = jnp.einsum('bthd,hg->btghd', q4, eye).reshape(b, N_QROW, W_A).astype(BF16)
    bias_rows = jnp.broadcast_to(jnp.tile(bias, N_DEC)[:, None], (N_QROW, LANES))
    new_t = lambda a: jnp.pad(jnp.swapaxes(a, 1, 2), ((0, 0), (0, 0), (0, PAGE - N_DEC)))

    def page_map(slot):
        def index(i, j, pt):
            return (layer, pt[i, n_pages - 1 - ((jnp.maximum(j, 1) - 1) * SB_PAGES + slot)], 0, 0, 0)
        return pl.BlockSpec((1, 1, H_A, D_A, PAGE), index)

    per_b = lambda shape: pl.BlockSpec((1,) + shape, lambda i, j, pt: (i, 0, 0))
    pages = [page_map(s) for s in range(SB_PAGES)]
    grid_spec = pltpu.PrefetchScalarGridSpec(
        num_scalar_prefetch=1,
        grid=(b, n_pages // SB_PAGES + 1),
        in_specs=[per_b((N_QROW, W_A)),
                  pl.BlockSpec((N_QROW, LANES), lambda i, j, pt: (0, 0)),
                  per_b((W_A, PAGE)), per_b((W_A, PAGE))] + pages + pages
                 + [pl.BlockSpec((2 * PAGE, 2 * LANES), lambda i, j, pt: (0, 0))],
        out_specs=per_b((N_DEC, W_A)),
        scratch_shapes=[pltpu.VMEM((N_QROW, W_A), F32), pltpu.VMEM((N_QROW, LANES), F32)],
    )
    return pl.pallas_call(
        _sb_sample_kernel,
        grid_spec=grid_spec,
        out_shape=jax.ShapeDtypeStruct((b, N_DEC, W_A), F32),
        compiler_params=_cparams(("parallel", "arbitrary")),
        name="sb_sample",
    )(page_table, qbd, bias_rows, new_t(k), new_t(v), *([cache_kt] * SB_PAGES), *([cache_vt] * SB_PAGES),
      _sb_tri())


def _mm(a, b, kind="nn", passes=1):
    dot = {"nn": _dot, "nt": _dot_nt, "tn": _dot_tn}[kind]
    if passes == 1:
        return dot(a.astype(BF16), b.astype(BF16))
    a_hi, a_lo = _split2(a)
    if passes == 2:
        b_hi = b.astype(BF16)
        return dot(a_hi, b_hi) + dot(a_lo, b_hi)
    b_hi, b_lo = _split2(b)
    return dot(a_hi, b_hi) + (dot(a_hi, b_lo) + dot(a_lo, b_hi))


def _head_ones(n):
    r = np.arange(n)[:, None] // D_R
    c = np.arange(n)[None, :] // D_R
    return jnp.asarray((r == c).astype(np.float32), dtype=BF16)


def _head_sum(x, ones):
    hi, mid, lo = _split3(x)
    return _dot(hi, ones) + (_dot(mid, ones) + _dot(lo, ones))


def _softplus(x):
    return jnp.maximum(x, 0.0) + _log1p_exp(x)


HG_SUB = 16


def _hgrn_kernel(q_ref, f_ref, i_ref, g_ref, la_ref, l1_ref, gn_ref, s0_ref, bt_ref, ones_ref,
                 o_ref, sf_ref, s_ref, *, valid):
    c = pl.program_id(1)

    @pl.when(c == 0)
    def _():
        s_ref[...] = s0_ref[0]

    tc = q_ref.shape[1]
    bq = q_ref[0]
    qs = bq * jax.nn.sigmoid(bq)
    bf = f_ref[0]
    la = la_ref[...]
    l1p = _log1p_exp(bf)
    lb = l1_ref[...] + (jnp.minimum(bf, 0.0) - l1p)
    lf = jnp.maximum(la, lb) + _log1p_exp(la - lb)
    kk = jnp.exp(l1_ref[...] - (jnp.maximum(bf, 0.0) + l1p))
    if valid < tc:
        live = lax.broadcasted_iota(jnp.int32, (tc, W_R), 0) < valid
        lf = jnp.where(live, lf, 0.0)
        kk = jnp.where(live, kk, 0.0)
    hi, mid, lo = _split3(lf)
    cum = _dot(bt_ref[...], hi) + (_dot(bt_ref[...], mid) + _dot(bt_ref[...], lo))
    iv = i_ref[0]

    trow = lax.broadcasted_iota(jnp.int32, (HG_SUB, LANES), 0)
    r2 = lax.broadcasted_iota(jnp.int32, (LANES, LANES), 0) // D_R
    c2 = lax.broadcasted_iota(jnp.int32, (LANES, LANES), 1) // D_R
    same_head = r2 == c2
    ones_pair = ones_ref[0:LANES, 0:LANES]

    n_sub = tc // HG_SUB
    units = [(sc, p) for sc in range(n_sub) for p in range(2)]
    part = lambda x, sc, p: x[sc * HG_SUB:(sc + 1) * HG_SUB, p * LANES:(p + 1) * LANES]

    scores, upd, q_dec, decay, ivs = {}, {}, {}, {}, {}
    for u in units:
        cumc, qc, kc, ic = (part(x, *u) for x in (cum, qs, kk, iv))
        last = cumc[HG_SUB - 1:HG_SUB]
        ps = []
        for j in range(HG_SUB):
            d = jnp.where(trow >= j, cumc - cumc[j:j + 1], NEG_BIG)
            ps.append(jnp.exp(d) * qc * kc[j:j + 1])
        scores[u] = _dot(jnp.concatenate(ps, axis=0).astype(BF16), ones_pair)
        upd[u] = _dot_tn(ic.astype(BF16), (kc * jnp.exp(last - cumc)).astype(BF16))
        q_dec[u] = (qc * jnp.exp(cumc)).astype(BF16)
        decay[u] = jnp.exp(last)
        ivs[u] = ic

    s = [s_ref[0], s_ref[1]]
    inter = {}
    for u in units:
        p = u[1]
        inter[u] = _dot_nt(q_dec[u], s[p].astype(BF16))
        s[p] = s[p] * decay[u] + jnp.where(same_head, upd[u], 0.0)
    s_ref[0] = s[0]
    s_ref[1] = s[1]

    outs = {}
    for u in units:
        intra = scores[u][0:HG_SUB] * ivs[u][0:1]
        for j in range(1, HG_SUB):
            intra = intra + scores[u][j * HG_SUB:(j + 1) * HG_SUB] * ivs[u][j:j + 1]
        outs[u] = inter[u] + intra
    o = jnp.concatenate([jnp.concatenate([outs[(sc, p)] for p in range(2)], axis=1) for sc in range(n_sub)],
                        axis=0)
    ms = _head_sum(o * o, ones_ref[...]) * (1.0 / D_R)
    bg = g_ref[0]
    o_ref[0] = (o * lax.rsqrt(ms + RMS_EPS)) * gn_ref[...] * (bg * jax.nn.sigmoid(bg))

    @pl.when(c == pl.num_programs(1) - 1)
    def _():
        sf_ref[0] = s_ref[...]


def _hgrn(bslab, log_lb, log1m_lb, gnorm, s0bd, valid):
    b, l, _ = bslab.shape
    tc = min(l, 256)
    col = lambda k: pl.BlockSpec((1, tc, W_R), lambda i, c: (i, c, k))
    par = _const_spec((1, W_R))
    st = pl.BlockSpec((1, 2, LANES, LANES), lambda i, c: (i, 0, 0, 0))
    t = np.arange(tc)[:, None]
    s = np.arange(tc)[None, :]
    blk_tri = jnp.asarray(((t // HG_SUB == s // HG_SUB) & (s <= t)).astype(np.float32), dtype=BF16)
    return pl.pallas_call(
        functools.partial(_hgrn_kernel, valid=valid),
        grid=(b, l // tc),
        in_specs=[col(0), col(1), col(2), col(3), par, par, par, st,
                  _const_spec((tc, tc)), _const_spec((W_R, W_R))],
        out_specs=[pl.BlockSpec((1, tc, W_R), lambda i, c: (i, c, 0)), st],
        out_shape=[jax.ShapeDtypeStruct((b, l, W_R), F32),
                   jax.ShapeDtypeStruct((b, 2, LANES, LANES), F32)],
        scratch_shapes=[pltpu.VMEM((2, LANES, LANES), F32)],
        compiler_params=_cparams(("parallel", "arbitrary")),
        name="hgrn",
    )(bslab, bslab, bslab, bslab, log_lb.reshape(1, W_R), log1m_lb.reshape(1, W_R),
      gnorm.reshape(1, W_R), s0bd, blk_tri, _head_ones(W_R))


def _pair_state_in(s):
    b = s.shape[0]
    st = jnp.swapaxes(s, 2, 3).reshape(b, 2, 2, D_R, D_R)
    return jnp.einsum('bpevk,ef->bpevfk', st, jnp.eye(2, dtype=s.dtype)).reshape(b, 2, LANES, LANES)


def _pair_state_out(sbd):
    b = sbd.shape[0]
    s6 = sbd.reshape(b, 2, 2, D_R, 2, D_R)
    st = jnp.stack([s6[:, :, 0, :, 0, :], s6[:, :, 1, :, 1, :]], axis=2)
    return jnp.swapaxes(st.reshape(b, H_R, D_R, D_R), 2, 3)


RW_CHUNK = 64
RW_N = H_R * RW_CHUNK


RW_SEQS = 8


def _each(f, *lists):
    return [f(*args) for args in zip(*lists)]


def _rwkv_kernel(c_ref, sh0_ref, s0_ref, mu_ref, w0_ref, wup_ref, a0_ref, aup_ref, gup_ref,
                 kk_ref, ka_ref, rk_ref, lnw_ref, lnb_ref, tri_ref, ones_ref,
                 o_ref, sf_ref, prev_ref, s_ref, *, valid, passes):
    c = pl.program_id(1)

    @pl.when(c == 0)
    def _():
        prev_ref[...] = sh0_ref[...]
        s_ref[...] = s0_ref[...]

    seqs = list(range(RW_SEQS))
    cb = [c_ref[n] for n in seqs]
    tok = lax.broadcasted_iota(jnp.int32, (RW_CHUNK, C_PROJ), 0)
    shifted = [jnp.where(tok == 0, prev_ref[n], pltpu.roll(cb[n], 1, axis=0)) for n in seqs]
    for n in seqs:
        prev_ref[n] = cb[n][RW_CHUNK - 1:RW_CHUNK]
    xm = _each(lambda x, sh: x + (sh - x) * mu_ref[...], cb, shifted)
    r = [x[:, 0:W_R] for x in xm]
    k = [x[:, W_R:2 * W_R] for x in xm]
    v = [x[:, 2 * W_R:3 * W_R] for x in xm]
    tail = [x[:, 3 * W_R:C_PROJ] for x in xm]
    ones = ones_ref[...]

    w_log = _each(lambda t: -_softplus(-(w0_ref[...] + _mm(jnp.tanh(t), wup_ref[...]))) - 0.5, tail)
    lw = [-jnp.exp(x) for x in w_log]
    a = _each(lambda t: jax.nn.sigmoid(a0_ref[...] + _mm(t, aup_ref[...])), tail)
    g = _each(lambda t: _mm(jax.nn.sigmoid(t), gup_ref[...]), tail)
    kk = [x * kk_ref[...] for x in k]
    kk = _each(lambda x: x / jnp.maximum(jnp.sqrt(_head_sum(x * x, ones)), L2_EPS), kk)
    k = _each(lambda x, aa: x * (1.0 + (aa - 1.0) * ka_ref[...]), k, a)
    av = [-x for x in kk]
    bv = _each(lambda x, aa: x * aa, kk, a)
    if valid < RW_CHUNK:
        live = lax.broadcasted_iota(jnp.int32, (RW_CHUNK, W_R), 0) < valid
        zero = lambda xs: [jnp.where(live, x, 0.0) for x in xs]
        lw, av, bv, k, v = zero(lw), zero(av), zero(bv), zero(k), zero(v)

    def cumsum(x):
        hi, mid, lo = _split3(x)
        return _dot(tri_ref[...], hi) + (_dot(tri_ref[...], mid) + _dot(tri_ref[...], lo))

    cl = _each(cumsum, lw)
    last = [x[RW_CHUNK - 1:RW_CHUNK] for x in cl]
    e_neg = [jnp.exp(-x) for x in cl]
    e_end = _each(lambda l, x: jnp.exp(l - x), last, cl)

    lane_head = lax.broadcasted_iota(jnp.int32, (RW_CHUNK, W_R), 1) // D_R

    def stack(x):
        return jnp.concatenate([jnp.where(lane_head == h, x, 0.0) for h in range(H_R)], axis=0)

    m_a = _each(lambda x, c_, l_: stack(x * jnp.exp(c_ - l_)), av, cl, lw)
    m_b = _each(lambda x, e: stack(x * e), bv, e_neg)
    m_k = _each(lambda x, e: stack(x * e), k, e_neg)
    m_r = _each(lambda x, c_: stack(x * jnp.exp(c_)), r, cl)
    m_v = _each(stack, v)
    m_bh = _each(lambda x, e: stack(x * e), bv, e_end)
    m_kh = _each(lambda x, e: stack(x * e), k, e_end)

    row = lax.broadcasted_iota(jnp.int32, (RW_N, RW_N), 0)
    col = lax.broadcasted_iota(jnp.int32, (RW_N, RW_N), 1)
    mm = functools.partial(_mm, passes=passes)
    nt = functools.partial(mm, kind="nt")
    tn = functools.partial(mm, kind="tn")
    add = lambda x, y: x + y
    t_ab = _each(lambda x, y: jnp.where(row > col, nt(x, y), 0.0), m_a, m_b)
    t_ak = _each(lambda x, y: jnp.where(row > col, nt(x, y), 0.0), m_a, m_k)
    t_rb = _each(lambda x, y: jnp.where(row >= col, nt(x, y), 0.0), m_r, m_b)
    t_rk = _each(lambda x, y: jnp.where(row >= col, nt(x, y), 0.0), m_r, m_k)

    inv = [jnp.where(row == col, 1.0, t) for t in t_ab]
    pw = t_ab
    for _ in range(5):
        pw = _each(mm, pw, pw)
        inv = _each(add, inv, _each(mm, inv, pw))

    w1 = _each(mm, inv, m_a)
    w2 = _each(mm, inv, _each(mm, t_ak, m_v))
    rm = _each(add, m_r, _each(mm, t_rb, w1))
    yc = _each(add, _each(mm, t_rb, w2), _each(mm, t_rk, m_v))
    pm = _each(lambda l, x, y: jnp.where(row == col, jnp.exp(l), 0.0) + tn(x, y), last, w1, m_bh)
    qm = _each(add, _each(tn, w2, m_bh), _each(tn, m_v, m_kh))

    s = [s_ref[n] for n in seqs]
    y_st = _each(add, _each(nt, rm, s), yc)
    s_new = _each(lambda x, p_, q_: _mm(x, p_, passes=max(passes, 2)) + q_, s, pm, qm)
    for n in seqs:
        s_ref[n] = s_new[n]
    y = [(t[0:RW_CHUNK] + t[RW_CHUNK:2 * RW_CHUNK]) + (t[2 * RW_CHUNK:3 * RW_CHUNK] + t[3 * RW_CHUNK:])
         for t in y_st]

    mean = [_head_sum(x, ones) * (1.0 / D_R) for x in y]
    yc0 = _each(lambda x, m: x - m, y, mean)
    var = [_head_sum(x * x, ones) * (1.0 / D_R) for x in yc0]
    yn = _each(lambda x, vv: x * lax.rsqrt(vv + GN_EPS) * lnw_ref[...] + lnb_ref[...], yc0, var)
    bonus = _each(lambda rr, kk_, vv: _head_sum(rr * kk_ * rk_ref[...], ones) * vv, r, k, v)
    for n in seqs:
        o_ref[n] = (yn[n] + bonus[n]) * g[n]

    @pl.when(c == pl.num_programs(1) - 1)
    def _():
        sf_ref[...] = s_ref[...]


def _rwkv(cslab, shift0, s0bd, p, valid, passes=1):
    b, l, _ = cslab.shape
    row = lambda a: a.reshape(1, -1)
    pad_up = lambda w, off: jnp.zeros((LANES, W_R), F32).at[off:off + w.shape[0]].set(w).astype(BF16)
    t = np.arange(RW_CHUNK)[:, None]
    s = np.arange(RW_CHUNK)[None, :]
    consts = [row(p['rwkv_mu']), row(p['rwkv_w0']), pad_up(p['rwkv_w_up'], 0), row(p['rwkv_a0']),
              pad_up(p['rwkv_a_up'], 32), pad_up(p['rwkv_g_up'], 64), row(p['rwkv_k_k']),
              row(p['rwkv_k_a']), row(p['rwkv_r_k']), row(p['rwkv_ln_w']), row(p['rwkv_ln_b']),
              jnp.asarray((s <= t).astype(np.float32), dtype=BF16), _head_ones(W_R)]
    assert b % RW_SEQS == 0
    st = pl.BlockSpec((RW_SEQS, RW_N, RW_N), lambda i, c: (i, 0, 0))
    return pl.pallas_call(
        functools.partial(_rwkv_kernel, valid=valid, passes=passes),
        grid=(b // RW_SEQS, l // RW_CHUNK),
        in_specs=[pl.BlockSpec((RW_SEQS, RW_CHUNK, C_PROJ), lambda i, c: (i, c, 0)),
                  pl.BlockSpec((RW_SEQS, 1, C_PROJ), lambda i, c: (i, 0, 0)), st]
                 + [_const_spec(a.shape) for a in consts],
        out_specs=[pl.BlockSpec((RW_SEQS, RW_CHUNK, W_R), lambda i, c: (i, c, 0)), st],
        out_shape=[jax.ShapeDtypeStruct((b, l, W_R), F32),
                   jax.ShapeDtypeStruct((b, RW_N, RW_N), F32)],
        scratch_shapes=[pltpu.VMEM((RW_SEQS, 1, C_PROJ), F32), pltpu.VMEM((RW_SEQS, RW_N, RW_N), F32)],
        compiler_params=_cparams(("parallel", "arbitrary")),
        name="rwkv",
    )(cslab, shift0, s0bd, *consts)


def _quad_state_in(s):
    b = s.shape[0]
    return jnp.einsum('bhij,hg->bhigj', s, jnp.eye(H_R, dtype=s.dtype)).reshape(b, RW_N, RW_N)


def _quad_state_out(sbd):
    b = sbd.shape[0]
    s5 = sbd.reshape(b, H_R, D_R, H_R, D_R)
    return jnp.stack([s5[:, h, :, h, :] for h in range(H_R)], axis=1)


def _pad_tokens(a, multiple):
    pad = (-a.shape[1]) % multiple
    return a if pad == 0 else jnp.pad(a, ((0, 0), (0, pad), (0, 0)))


def _trunk_layer(x, b, l, sb_fn, s_hgrn0, s_rwkv0, shift0, mem_kv, lb, p, final_gain, prompt_kv=None):
    if prompt_kv is not None:
        layer, depth, kv_all = prompt_kv
        q, k, v, bslab, cslab = _in_proj_t(x, p['norm_mix'], p['w_in'], b, l, layer, depth, kv_all)
        o_a = sb_fn(q.reshape(b, l, W_A), k, v, layer=layer)
    else:
        q, k, v, bslab, cslab = _in_proj(x, p['norm_mix'], p['w_in'])
        o_a = sb_fn(q.reshape(b, l, W_A), k.reshape(b, l, W_A), v.reshape(b, l, W_A))
        k, v = k.reshape(b, l, H_A, D_A), v.reshape(b, l, H_A, D_A)
    o_b, s_hgrn = _hgrn(_pad_tokens(bslab.reshape(b, l, 4 * W_R), HG_SUB), jnp.log(lb), jnp.log1p(-lb),
                        p['hgrn_norm'], _pair_state_in(s_hgrn0), l)
    cslab = cslab.reshape(b, l, C_PROJ)
    o_c, s_rwkv = _rwkv(_pad_tokens(cslab, RW_CHUNK), shift0.reshape(b, 1, C_PROJ),
                        _quad_state_in(s_rwkv0), p, l)
    x1, qm = _out_q(x, o_a.reshape(b * l, W_A), o_b[:, :l].reshape(b * l, W_R),
                    o_c[:, :l].reshape(b * l, W_R), p['w_out'], p['norm_mem'], p['mem_wq'])
    x2 = _mem_attn(x1.reshape(b, l, D_MODEL), qm.reshape(b, l, D_MODEL), *mem_kv, p['mem_wo'])
    x3 = _ffn(x2.reshape(b * l, D_MODEL), p['norm_ffn'], p['ffn_w_gate'], p['ffn_w_up'], p['ffn_w_down'],
              p['norm_ffn'] if final_gain is None else final_gain, final_gain is not None)
    return x3, k, v, _pair_state_out(s_hgrn), _quad_state_out(s_rwkv), cslab[:, l - 1]


def kernel(x_prompt, x_sample, mem_prompt, cache_sb_k, cache_sb_v, page_table, state_hgrn, state_rwkv,
           state_rwkv_shift, cache_mem_k, cache_mem_v, norm_mix, w_in, sb_bias, hgrn_lb, hgrn_norm, rwkv_mu,
           rwkv_w0, rwkv_w_up, rwkv_a0, rwkv_a_up, rwkv_g_up, rwkv_k_k, rwkv_k_a, rwkv_r_k, rwkv_ln_w, rwkv_ln_b,
           w_out, norm_mem, mem_wq, mem_wk, mem_wv, mem_wo, norm_ffn, ffn_w_gate, ffn_w_up, ffn_w_down,
           final_norm):
    bp, lp, _ = x_prompt.shape
    bs, ls, _ = x_sample.shape
    depth = w_in.shape[0]
    n_pool = cache_sb_k.shape[1]
    lb_all = jnp.cumsum(jax.nn.softmax(hgrn_lb.astype(F32), axis=0), axis=0)
    lb_all = lb_all - lb_all[0]
    assert page_table.shape[1] % SB_PAGES == 0 and x_sample.shape[1] == N_DEC
    cache_kt = jnp.transpose(cache_sb_k, (0, 1, 3, 4, 2))
    cache_vt = jnp.transpose(cache_sb_v, (0, 1, 3, 4, 2))
    mem_tok = mem_prompt.reshape(bp * N_MEM, D_MODEL)
    mem_k_s = cache_mem_k.reshape(depth, bs, N_MEM, D_MODEL)
    mem_v_s = cache_mem_v.reshape(depth, bs, N_MEM, D_MODEL)
    xp = x_prompt.reshape(bp * lp, D_MODEL)
    xs = x_sample.reshape(bs * ls, D_MODEL)
    outs = [[] for _ in range(10)]
    kv_p = None
    for l in range(depth):
        bf = lambda w: w[l].astype(BF16)
        p = dict(norm_mix=norm_mix[l], w_in=bf(w_in), hgrn_norm=hgrn_norm[l], rwkv_mu=rwkv_mu[l],
                 rwkv_w0=rwkv_w0[l], rwkv_w_up=rwkv_w_up[l], rwkv_a0=rwkv_a0[l], rwkv_a_up=rwkv_a_up[l],
                 rwkv_g_up=rwkv_g_up[l], rwkv_k_k=rwkv_k_k[l], rwkv_k_a=rwkv_k_a[l], rwkv_r_k=rwkv_r_k[l],
                 rwkv_ln_w=rwkv_ln_w[l], rwkv_ln_b=rwkv_ln_b[l], w_out=bf(w_out), norm_mem=norm_mem[l],
                 mem_wq=bf(mem_wq), mem_wo=bf(mem_wo), norm_ffn=norm_ffn[l], ffn_w_gate=bf(ffn_w_gate),
                 ffn_w_up=bf(ffn_w_up), ffn_w_down=bf(ffn_w_down))
        gain = final_norm if l == depth - 1 else None
        bias = sb_bias[l]
        mk_p = _matmul(mem_tok, bf(mem_wk)).reshape(bp, N_MEM, D_MODEL)
        mv_p = _matmul(mem_tok, bf(mem_wv)).reshape(bp, N_MEM, D_MODEL)
        xp, kt_p, vt_p, hg_p, rw_p, sh_p = _trunk_layer(
            xp, bp, lp, functools.partial(_sb_prompt, bias=bias),
            jnp.zeros((bp, H_R, D_R, D_R), F32), jnp.zeros((bp, H_R, D_R, D_R), F32),
            jnp.zeros((bp, C_PROJ), F32), (mk_p[None], mv_p[None], 0), lb_all[l], p, gain, (l, depth, kv_p))
        kv_p = (kt_p, vt_p)
        sb_fn = functools.partial(_sb_sample, bias=bias, cache_kt=cache_kt, cache_vt=cache_vt, layer=l,
                                  page_table=page_table)
        xs, ks, vs, hg_s, rw_s, sh_s = _trunk_layer(
            xs, bs, ls, sb_fn, state_hgrn[l], state_rwkv[l], state_rwkv_shift[l],
            (mem_k_s, mem_v_s, l), lb_all[l], p, gain)
        for lst, val in zip(outs, (ks, vs, hg_p, hg_s, rw_p, rw_s, sh_p, sh_s,
                                   mk_p.reshape(bp, N_MEM, H_M, D_M), mv_p.reshape(bp, N_MEM, H_M, D_M))):
            lst.append(val)
    heads_last = lambda a: jnp.transpose(a.reshape(depth, bp, H_A, D_A, lp), (0, 1, 4, 2, 3))
    return ((xp.reshape(bp, lp, D_MODEL), xs.reshape(bs, ls, D_MODEL), heads_last(kv_p[0]), heads_last(kv_p[1]))
            + tuple(jnp.stack(o) for o in outs))
```

```python
import functools

import jax
import jax.numpy as jnp
from jax import lax
from jax.experimental import pallas as pl
from jax.experimental.pallas import tpu as pltpu
import numpy as np

F32 = jnp.float32
BF16 = jnp.bfloat16

D_MODEL = 1024
H_A, D_A = 8, 64
W_A = H_A * D_A
H_R, D_R = 4, 64
W_R = H_R * D_R
H_M, D_M = 4, 256
N_MEM = 256
C_PROJ = 3 * W_R + 128
D_IN = 3 * W_A + 4 * W_R + C_PROJ
PAGE = 128
RMS_EPS = 1e-6
GN_EPS = 64e-5
L2_EPS = 1e-12

LANES = 128
SUBLANES = 8
VMEM_LIMIT = 56 * 1024 * 1024

NEG_BIG = -1e30


def _cparams(sem):
    return pltpu.CompilerParams(dimension_semantics=sem, vmem_limit_bytes=VMEM_LIMIT)


def _dot(a, b):
    return jnp.dot(a, b, preferred_element_type=F32)


def _dot_nt(a, b):
    return lax.dot_general(a, b, (((1,), (1,)), ((), ())), preferred_element_type=F32)


def _dot_tn(a, b):
    return lax.dot_general(a, b, (((0,), (0,)), ((), ())), preferred_element_type=F32)


def _split2(x):
    hi = x.astype(BF16)
    lo = (x - hi.astype(F32)).astype(BF16)
    return hi, lo


def _split3(x):
    hi = x.astype(BF16)
    r = x - hi.astype(F32)
    mid = r.astype(BF16)
    lo = (r - mid.astype(F32)).astype(BF16)
    return hi, mid, lo


def _rms(x, g):
    ms = jnp.mean(x * x, axis=-1, keepdims=True)
    return (x * lax.rsqrt(ms + RMS_EPS)) * g


def _log1p_exp(x):
    return jnp.log(1.0 + jnp.exp(-jnp.abs(x)))


def _const_spec(shape):
    nd = len(shape)
    return pl.BlockSpec(shape, lambda *_: (0,) * nd)


def _in_proj_kernel(x_ref, g_ref, w_ref, q_ref, k_ref, v_ref, b_ref, c_ref):
    hb = _rms(x_ref[...], g_ref[...]).astype(BF16)
    q_ref[...] = _dot(hb, w_ref[:, 0:W_A])
    k_ref[...] = _dot(hb, w_ref[:, W_A:2 * W_A])
    v_ref[...] = _dot(hb, w_ref[:, 2 * W_A:3 * W_A])
    b_ref[...] = _dot(hb, w_ref[:, 3 * W_A:3 * W_A + 4 * W_R])
    c_ref[...] = _dot(hb, w_ref[:, 3 * W_A + 4 * W_R:D_IN])


def _in_proj(x, g, w_bf):
    m = x.shape[0]
    tm = min(m, 512)
    widths = (W_A, W_A, W_A, 4 * W_R, C_PROJ)
    return pl.pallas_call(
        _in_proj_kernel,
        grid=(m // tm,),
        in_specs=[pl.BlockSpec((tm, D_MODEL), lambda i: (i, 0)),
                  _const_spec((1, D_MODEL)),
                  _const_spec((D_MODEL, D_IN))],
        out_specs=[pl.BlockSpec((tm, w), lambda i: (i, 0)) for w in widths],
        out_shape=[jax.ShapeDtypeStruct((m, w), F32) for w in widths],
        compiler_params=_cparams(("parallel",)),
        name="in_proj",
    )(x, g.reshape(1, D_MODEL), w_bf)


def _in_proj_t_kernel(x_ref, g_ref, w_ref, wkvt_ref, *rest):
    q_ref, kt_ref, vt_ref, b_ref, c_ref = rest[-5:]
    hb = _rms(x_ref[...], g_ref[...]).astype(BF16)
    q_ref[...] = _dot(hb, w_ref[:, 0:W_A])
    kt_ref[0, 0] = _dot_nt(wkvt_ref[0:W_A, :], hb)
    vt_ref[0, 0] = _dot_nt(wkvt_ref[W_A:2 * W_A, :], hb)
    b_ref[...] = _dot(hb, w_ref[:, 3 * W_A:3 * W_A + 4 * W_R])
    c_ref[...] = _dot(hb, w_ref[:, 3 * W_A + 4 * W_R:D_IN])


def _in_proj_t(x, g, w_bf, b, l, layer, depth, kv_all):
    m = x.shape[0]
    tm = min(l, 512)
    assert l % tm == 0
    per_seq = l // tm
    wkvt = w_bf[:, W_A:3 * W_A].T
    row = lambda w: pl.BlockSpec((tm, w), lambda i: (i, 0))
    tr = pl.BlockSpec((1, 1, W_A, tm), lambda i: (layer, i // per_seq, 0, i % per_seq))
    carried = [] if kv_all is None else list(kv_all)
    return pl.pallas_call(
        _in_proj_t_kernel,
        grid=(m // tm,),
        in_specs=[row(D_MODEL), _const_spec((1, D_MODEL)), _const_spec((D_MODEL, D_IN)),
                  _const_spec((2 * W_A, D_MODEL))] + [pl.BlockSpec(memory_space=pl.ANY)] * len(carried),
        out_specs=[row(W_A), tr, tr, row(4 * W_R), row(C_PROJ)],
        out_shape=[jax.ShapeDtypeStruct((m, W_A), F32),
                   jax.ShapeDtypeStruct((depth, b, W_A, l), F32),
                   jax.ShapeDtypeStruct((depth, b, W_A, l), F32),
                   jax.ShapeDtypeStruct((m, 4 * W_R), F32),
                   jax.ShapeDtypeStruct((m, C_PROJ), F32)],
        input_output_aliases={4: 1, 5: 2} if carried else {},
        compiler_params=_cparams(("parallel",)),
        name="in_proj_t",
    )(x, g.reshape(1, D_MODEL), w_bf, wkvt, *carried)


def _matmul_kernel(x_ref, w_ref, o_ref):
    o_ref[...] = _dot(x_ref[...].astype(BF16), w_ref[...])


def _matmul(x, w_bf):
    m, k = x.shape
    n = w_bf.shape[1]
    tm = min(m, 512)
    return pl.pallas_call(
        _matmul_kernel,
        grid=(m // tm,),
        in_specs=[pl.BlockSpec((tm, k), lambda i: (i, 0)), _const_spec((k, n))],
        out_specs=pl.BlockSpec((tm, n), lambda i: (i, 0)),
        out_shape=jax.ShapeDtypeStruct((m, n), F32),
        compiler_params=_cparams(("parallel",)),
        name="matmul",
    )(x, w_bf)


def _out_q_kernel(x_ref, oa_ref, ob_ref, oc_ref, wo_ref, g_ref, wq_ref, x1_ref, qm_ref):
    x1 = (x_ref[...]
          + _dot(oa_ref[...].astype(BF16), wo_ref[0:W_A, :])
          + _dot(ob_ref[...].astype(BF16), wo_ref[W_A:W_A + W_R, :])
          + _dot(oc_ref[...].astype(BF16), wo_ref[W_A + W_R:W_A + 2 * W_R, :]))
    x1_ref[...] = x1
    qm_ref[...] = _dot(_rms(x1, g_ref[...]).astype(BF16), wq_ref[...])


def _out_q(x, oa, ob, oc, wo_bf, g, wq_bf):
    m = x.shape[0]
    tm = min(m, 512)
    row = lambda w: pl.BlockSpec((tm, w), lambda i: (i, 0))
    return pl.pallas_call(
        _out_q_kernel,
        grid=(m // tm,),
        in_specs=[row(D_MODEL), row(W_A), row(W_R), row(W_R),
                  _const_spec((D_MODEL, D_MODEL)), _const_spec((1, D_MODEL)),
                  _const_spec((D_MODEL, D_MODEL))],
        out_specs=[row(D_MODEL), row(D_MODEL)],
        out_shape=[jax.ShapeDtypeStruct((m, D_MODEL), F32)] * 2,
        compiler_params=_cparams(("parallel",)),
        name="out_q",
    )(x, oa, ob, oc, wo_bf, g.reshape(1, D_MODEL), wq_bf)


def _mem_attn_kernel(x_ref, q_ref, mk_ref, mv_ref, wo_ref, o_ref):
    sls = [slice(h * D_M, (h + 1) * D_M) for h in range(H_M)]
    qs = [(q_ref[0, :, sl] * (D_M ** -0.5)).astype(BF16) for sl in sls]
    ss = [_dot_nt(q, mk_ref[0, 0, :, sl].astype(BF16)) for q, sl in zip(qs, sls)]
    es = [jnp.exp(s - jnp.max(s, axis=-1, keepdims=True)) for s in ss]
    ps = [(e / jnp.sum(e, axis=-1, keepdims=True)).astype(BF16) for e in es]
    ohs = [_dot(p, mv_ref[0, 0, :, sl].astype(BF16)).astype(BF16) for p, sl in zip(ps, sls)]
    acc = x_ref[0]
    for oh, sl in zip(ohs, sls):
        acc = acc + _dot(oh, wo_ref[sl, :])
    o_ref[0] = acc


def _mem_attn(x1, qm, mk, mv, group, wo_bf):
    b, l, _ = x1.shape
    tl = min(l, 512)
    tok = pl.BlockSpec((1, tl, D_MODEL), lambda i, j: (i, j, 0))
    mem = pl.BlockSpec((1, 1, N_MEM, D_MODEL), lambda i, j: (group, i, 0, 0))
    return pl.pallas_call(
        _mem_attn_kernel,
        grid=(b, l // tl),
        in_specs=[tok, tok, mem, mem, _const_spec((D_MODEL, D_MODEL))],
        out_specs=tok,
        out_shape=jax.ShapeDtypeStruct((b, l, D_MODEL), F32),
        compiler_params=_cparams(("parallel", "parallel")),
        name="mem_attn",
    )(x1, qm, mk, mv, wo_bf)


FF_CHUNK = 256


def _ffn_kernel(x_ref, g_ref, wg_ref, wu_ref, wd_ref, gf_ref, o_ref, acc_ref, *, final_norm):
    x = x_ref[...]
    hb = _rms(x, g_ref[...]).astype(BF16)
    acc_ref[...] = x
    d_ff = wg_ref.shape[1]
    for c in range(d_ff // FF_CHUNK):
        sl = slice(c * FF_CHUNK, (c + 1) * FF_CHUNK)
        gate = _dot(hb, wg_ref[:, sl])
        up = _dot(hb, wu_ref[:, sl])
        a = (gate * jax.nn.sigmoid(gate)) * up
        acc_ref[...] += _dot(a.astype(BF16), wd_ref[sl, :])
    if final_norm:
        o_ref[...] = _rms(acc_ref[...], gf_ref[...])
    else:
        o_ref[...] = acc_ref[...]


def _ffn(x, g, wg_bf, wu_bf, wd_bf, gf, final_norm):
    m = x.shape[0]
    d_ff = wg_bf.shape[1]
    tm = min(m, 512)
    row = pl.BlockSpec((tm, D_MODEL), lambda i: (i, 0))
    return pl.pallas_call(
        functools.partial(_ffn_kernel, final_norm=final_norm),
        grid=(m // tm,),
        in_specs=[row, _const_spec((1, D_MODEL)), _const_spec((D_MODEL, d_ff)),
                  _const_spec((D_MODEL, d_ff)), _const_spec((d_ff, D_MODEL)),
                  _const_spec((1, D_MODEL))],
        out_specs=row,
        out_shape=jax.ShapeDtypeStruct((m, D_MODEL), F32),
        scratch_shapes=[pltpu.VMEM((tm, D_MODEL), F32)],
        compiler_params=_cparams(("parallel",)),
        name="ffn",
    )(x, g.reshape(1, D_MODEL), wg_bf, wu_bf, wd_bf, gf.reshape(1, D_MODEL))


def _log_terms(z):
    nln = jnp.maximum(z, 0.0) + _log1p_exp(z)
    return nln, z - nln


def _split_trunc(x):
    hi = lax.bitcast_convert_type(lax.bitcast_convert_type(x, jnp.int32) & jnp.int32(-65536), F32)
    return hi.astype(BF16), (x - hi).astype(BF16)


SB_GROUP = 4


def _sb_prompt_kernel(bias_ref, q_ref, kt_ref, vt_ref, u_ref, o_ref, qs_ref, kb_ref, vs_ref, carry_ref,
                      ls_ref, hl_ref, w_ref):
    i = pl.program_id(1)
    nkb = kt_ref.shape[3] // PAGE
    lane = lax.broadcasted_iota(jnp.int32, (PAGE, LANES), 1)
    row = lax.broadcasted_iota(jnp.int32, (PAGE, LANES), 0)
    even = lane < D_A
    even_row = row < D_A
    causal = lane < row

    @pl.when(i == 0)
    def _():
        for j in range(nkb):
            cols = slice(j * PAGE, (j + 1) * PAGE)
            kb_ref[j] = kt_ref[0, 0, :, cols].astype(BF16)
            for p in range(H_A // 2):
                v2 = vt_ref[0, 0, p * LANES:(p + 1) * LANES, cols]
                vs_ref[p, j, :, 0:PAGE] = jnp.where(even_row, v2, 0.0).astype(BF16)
                vs_ref[p, j, :, PAGE:2 * PAGE] = jnp.where(even_row, 0.0, v2).astype(BF16)

    for p in range(H_A // 2):
        q2 = q_ref[0, :, p * LANES:(p + 1) * LANES] * (D_A ** -0.5)
        qs_ref[p, 0:PAGE, :] = jnp.where(even, q2, 0.0).astype(BF16)
        qs_ref[p, PAGE:2 * PAGE, :] = jnp.where(even, 0.0, q2).astype(BF16)

    def blocks(js, diag):
        for n, j in enumerate(js):
            for p in range(H_A // 2):
                s2 = _dot(qs_ref[p], kb_ref[j, p * LANES:(p + 1) * LANES, :])
                for e in range(2):
                    h = 2 * p + e
                    z = s2[e * PAGE:(e + 1) * PAGE] + bias_ref[h]
                    nln, ls = _log_terms(z)
                    if diag and n == 0:
                        nln = jnp.where(causal, nln, 0.0)
                    hi, lo = _split_trunc(nln)
                    ls_ref[n, h] = ls
                    hl_ref[n, h, :, 0:PAGE] = hi
                    hl_ref[n, h, :, PAGE:2 * PAGE] = lo
        for n, j in enumerate(js):
            for h in range(H_A):
                r = _dot(hl_ref[n, h], u_ref[...])
                between, tot = r[:, 0:LANES], r[:, LANES:2 * LANES]
                if diag and n == 0:
                    w = jnp.where(causal, jnp.exp(ls_ref[n, h] + between), 0.0)
                    carry_ref[h] = tot
                else:
                    c = carry_ref[h]
                    w = jnp.exp(ls_ref[n, h] + between + c)
                    carry_ref[h] = c + tot
                w_ref[n, h // 2, :, (h % 2) * PAGE:(h % 2 + 1) * PAGE] = w.astype(BF16)
        for p in range(H_A // 2):
            pv = _dot_nt(w_ref[0, p], vs_ref[p, js[0]])
            for n in range(1, len(js)):
                pv = pv + _dot_nt(w_ref[n, p], vs_ref[p, js[n]])
            if diag:
                o_ref[0, :, p * LANES:(p + 1) * LANES] = pv
            else:
                o_ref[0, :, p * LANES:(p + 1) * LANES] += pv

    blocks([i], True)
    rem = i % SB_GROUP
    for r in range(1, SB_GROUP):
        @pl.when(rem == r)
        def _(r=r):
            blocks([i - 1 - n for n in range(r)], False)

    def older(t, _):
        j = i - 1 - rem - SB_GROUP * t
        blocks([j - n for n in range(SB_GROUP)], False)
        return 0

    lax.fori_loop(0, i // SB_GROUP, older, 0)


def _sb_tri():
    j = np.arange(2 * PAGE)[:, None] % PAGE
    s = np.arange(2 * LANES)[None, :]
    return jnp.asarray(-((j > s) | (s >= LANES)).astype(np.float32), dtype=BF16)


def _sb_prompt(q, kt, vt, bias, layer=0):
    b, l, _ = q.shape
    nq = l // PAGE
    tok = pl.BlockSpec((1, PAGE, W_A), lambda i, j: (i, j, 0))
    seq = pl.BlockSpec((1, 1, W_A, l), lambda i, j: (layer, i, 0, 0))
    return pl.pallas_call(
        _sb_prompt_kernel,
        grid=(b, nq),
        in_specs=[pl.BlockSpec(memory_space=pltpu.SMEM), tok, seq, seq,
                  _const_spec((2 * PAGE, 2 * LANES))],
        out_specs=tok,
        out_shape=jax.ShapeDtypeStruct((b, l, W_A), F32),
        scratch_shapes=[pltpu.VMEM((H_A // 2, 2 * PAGE, LANES), BF16),
                        pltpu.VMEM((nq, W_A, PAGE), BF16),
                        pltpu.VMEM((H_A // 2, nq, LANES, 2 * PAGE), BF16),
                        pltpu.VMEM((H_A, PAGE, LANES), F32),
                        pltpu.VMEM((SB_GROUP, H_A, PAGE, LANES), F32),
                        pltpu.VMEM((SB_GROUP, H_A, PAGE, 2 * PAGE), BF16),
                        pltpu.VMEM((SB_GROUP, H_A // 2, PAGE, 2 * PAGE), BF16)],
        compiler_params=_cparams(("parallel", "arbitrary")),
        name="sb_prompt",
    )(bias, q, kt, vt, _sb_tri())


N_DEC = 4


SB_PAGES = 16
N_QROW = N_DEC * H_A


def _sb_sample_kernel(pt_ref, q_ref, br_ref, kn_ref, vn_ref, *rest):
    del pt_ref
    kp_refs, vp_refs = rest[:SB_PAGES], rest[SB_PAGES:2 * SB_PAGES]
    u_ref, o_ref, acc_ref, carry_ref = rest[2 * SB_PAGES:]
    j = pl.program_id(1)

    def logits(kt):
        z = _dot(q_ref[0], kt.astype(BF16)) + br_ref[...]
        return _log_terms(z)

    def suffix(nln):
        hi, lo = _split_trunc(nln)
        r = _dot(jnp.concatenate([hi, lo], axis=1), u_ref[...])
        return r[:, 0:LANES], r[:, LANES:2 * LANES]

    @pl.when(j == 0)
    def _():
        key = lax.broadcasted_iota(jnp.int32, (N_QROW, LANES), 1)
        tok = lax.broadcasted_iota(jnp.int32, (N_QROW, LANES), 0) // H_A
        visible = key < tok
        ln, ls = logits(kn_ref[0])
        ln = jnp.where(visible, ln, 0.0)
        between, tot = suffix(ln)
        w = jnp.where(visible, jnp.exp(ls + between), 0.0)
        carry_ref[...] = tot
        acc_ref[...] = _dot_nt(w.astype(BF16), vn_ref[0].astype(BF16))

    @pl.when(j > 0)
    def _():
        terms = [logits(kp_refs[i][0, 0].reshape(W_A, PAGE)) for i in range(SB_PAGES)]
        sums = [suffix(ln) for ln, _ in terms]
        c = carry_ref[...]
        ws = []
        for (_, ls), (between, tot) in zip(terms, sums):
            ws.append(jnp.exp(ls + between + c).astype(BF16))
            c = c + tot
        carry_ref[...] = c
        acc = acc_ref[...]
        for i in range(SB_PAGES):
            acc = acc + _dot_nt(ws[i], vp_refs[i][0, 0].reshape(W_A, PAGE).astype(BF16))
        acc_ref[...] = acc

    @pl.when(j == pl.num_programs(1) - 1)
    def _():
        head = lax.broadcasted_iota(jnp.int32, (H_A, W_A), 0)
        lane_head = lax.broadcasted_iota(jnp.int32, (H_A, W_A), 1) // D_A
        for t in range(N_DEC):
            rows = acc_ref[t * H_A:(t + 1) * H_A, :]
            o_ref[0, t:t + 1, :] = jnp.sum(jnp.where(head == lane_head, rows, 0.0), axis=0, keepdims=True)


def _sb_sample(q, k, v, bias, cache_kt, cache_vt, layer, page_table):
    b = q.shape[0]
    n_pages = page_table.shape[1]
    eye = jnp.eye(H_A, dtype=F32)
    q4 = q.reshape(b, N_DEC, H_A, D_A) * (D_A ** -0.5)
    qbd = jnp.einsum('bthd,hg->btghd', q4, eye).reshape(b, N_QROW, W_A).astype(BF16)
    bias_rows = jnp.broadcast_to(jnp.tile(bias, N_DEC)[:, None], (N_QROW, LANES))
    new_t = lambda a: jnp.pad(jnp.swapaxes(a, 1, 2), ((0, 0), (0, 0), (0, PAGE - N_DEC)))

    def page_map(slot):
        def index(i, j, pt):
            return (layer, pt[i, n_pages - 1 - ((jnp.maximum(j, 1) - 1) * SB_PAGES + slot)], 0, 0, 0)
        return pl.BlockSpec((1, 1, H_A, D_A, PAGE), index)

    per_b = lambda shape: pl.BlockSpec((1,) + shape, lambda i, j, pt: (i, 0, 0))
    pages = [page_map(s) for s in range(SB_PAGES)]
    grid_spec = pltpu.PrefetchScalarGridSpec(
        num_scalar_prefetch=1,
        grid=(b, n_pages // SB_PAGES + 1),
        in_specs=[per_b((N_QROW, W_A)),
                  pl.BlockSpec((N_QROW, LANES), lambda i, j, pt: (0, 0)),
                  per_b((W_A, PAGE)), per_b((W_A, PAGE))] + pages + pages
                 + [pl.BlockSpec((2 * PAGE, 2 * LANES), lambda i, j, pt: (0, 0))],
        out_specs=per_b((N_DEC, W_A)),
        scratch_shapes=[pltpu.VMEM((N_QROW, W_A), F32), pltpu.VMEM((N_QROW, LANES), F32)],
    )
    return pl.pallas_call(
        _sb_sample_kernel,
        grid_spec=grid_spec,
        out_shape=jax.ShapeDtypeStruct((b, N_DEC, W_A), F32),
        compiler_params=_cparams(("parallel", "arbitrary")),
        name="sb_sample",
    )(page_table, qbd, bias_rows, new_t(k), new_t(v), *([cache_kt] * SB_PAGES), *([cache_vt] * SB_PAGES),
      _sb_tri())


def _mm(a, b, kind="nn", passes=1):
    dot = {"nn": _dot, "nt": _dot_nt, "tn": _dot_tn}[kind]
    if passes == 1:
        return dot(a.astype(BF16), b.astype(BF16))
    a_hi, a_lo = _split2(a)
    if passes == 2:
        b_hi = b.astype(BF16)
        return dot(a_hi, b_hi) + dot(a_lo, b_hi)
    b_hi, b_lo = _split2(b)
    return dot(a_hi, b_hi) + (dot(a_hi, b_lo) + dot(a_lo, b_hi))


def _head_ones(n):
    r = np.arange(n)[:, None] // D_R
    c = np.arange(n)[None, :] // D_R
    return jnp.asarray((r == c).astype(np.float32), dtype=BF16)


def _head_sum(x, ones):
    hi, mid, lo = _split3(x)
    return _dot(hi, ones) + (_dot(mid, ones) + _dot(lo, ones))


def _softplus(x):
    return jnp.maximum(x, 0.0) + _log1p_exp(x)


HG_SUB = 16


def _hgrn_kernel(q_ref, f_ref, i_ref, g_ref, la_ref, l1_ref, gn_ref, s0_ref, bt_ref, ones_ref,
                 o_ref, sf_ref, s_ref, *, valid):
    c = pl.program_id(1)

    @pl.when(c == 0)
    def _():
        s_ref[...] = s0_ref[0]

    tc = q_ref.shape[1]
    bq = q_ref[0]
    qs = bq * jax.nn.sigmoid(bq)
    bf = f_ref[0]
    la = la_ref[...]
    l1p = _log1p_exp(bf)
    lb = l1_ref[...] + (jnp.minimum(bf, 0.0) - l1p)
    lf = jnp.maximum(la, lb) + _log1p_exp(la - lb)
    kk = jnp.exp(l1_ref[...] - (jnp.maximum(bf, 0.0) + l1p))
    if valid < tc:
        live = lax.broadcasted_iota(jnp.int32, (tc, W_R), 0) < valid
        lf = jnp.where(live, lf, 0.0)
        kk = jnp.where(live, kk, 0.0)
    hi, mid, lo = _split3(lf)
    cum = _dot(bt_ref[...], hi) + (_dot(bt_ref[...], mid) + _dot(bt_ref[...], lo))
    iv = i_ref[0]

    trow = lax.broadcasted_iota(jnp.int32, (HG_SUB, LANES), 0)
    r2 = lax.broadcasted_iota(jnp.int32, (LANES, LANES), 0) // D_R
    c2 = lax.broadcasted_iota(jnp.int32, (LANES, LANES), 1) // D_R
    same_head = r2 == c2
    ones_pair = ones_ref[0:LANES, 0:LANES]

    n_sub = tc // HG_SUB
    units = [(sc, p) for sc in range(n_sub) for p in range(2)]
    part = lambda x, sc, p: x[sc * HG_SUB:(sc + 1) * HG_SUB, p * LANES:(p + 1) * LANES]

    scores, upd, q_dec, decay, ivs = {}, {}, {}, {}, {}
    for u in units:
        cumc, qc, kc, ic = (part(x, *u) for x in (cum, qs, kk, iv))
        last = cumc[HG_SUB - 1:HG_SUB]
        ps = []
        for j in range(HG_SUB):
            d = jnp.where(trow >= j, cumc - cumc[j:j + 1], NEG_BIG)
            ps.append(jnp.exp(d) * qc * kc[j:j + 1])
        scores[u] = _dot(jnp.concatenate(ps, axis=0).astype(BF16), ones_pair)
        upd[u] = _dot_tn(ic.astype(BF16), (kc * jnp.exp(last - cumc)).astype(BF16))
        q_dec[u] = (qc * jnp.exp(cumc)).astype(BF16)
        decay[u] = jnp.exp(last)
        ivs[u] = ic

    s = [s_ref[0], s_ref[1]]
    inter = {}
    for u in units:
        p = u[1]
        inter[u] = _dot_nt(q_dec[u], s[p].astype(BF16))
        s[p] = s[p] * decay[u] + jnp.where(same_head, upd[u], 0.0)
    s_ref[0] = s[0]
    s_ref[1] = s[1]

    outs = {}
    for u in units:
        intra = scores[u][0:HG_SUB] * ivs[u][0:1]
        for j in range(1, HG_SUB):
            intra = intra + scores[u][j * HG_SUB:(j + 1) * HG_SUB] * ivs[u][j:j + 1]
        outs[u] = inter[u] + intra
    o = jnp.concatenate([jnp.concatenate([outs[(sc, p)] for p in range(2)], axis=1) for sc in range(n_sub)],
                        axis=0)
    ms = _head_sum(o * o, ones_ref[...]) * (1.0 / D_R)
    bg = g_ref[0]
    o_ref[0] = (o * lax.rsqrt(ms + RMS_EPS)) * gn_ref[...] * (bg * jax.nn.sigmoid(bg))

    @pl.when(c == pl.num_programs(1) - 1)
    def _():
        sf_ref[0] = s_ref[...]


def _hgrn(bslab, log_lb, log1m_lb, gnorm, s0bd, valid):
    b, l, _ = bslab.shape
    tc = min(l, 256)
    col = lambda k: pl.BlockSpec((1, tc, W_R), lambda i, c: (i, c, k))
    par = _const_spec((1, W_R))
    st = pl.BlockSpec((1, 2, LANES, LANES), lambda i, c: (i, 0, 0, 0))
    t = np.arange(tc)[:, None]
    s = np.arange(tc)[None, :]
    blk_tri = jnp.asarray(((t // HG_SUB == s // HG_SUB) & (s <= t)).astype(np.float32), dtype=BF16)
    return pl.pallas_call(
        functools.partial(_hgrn_kernel, valid=valid),
        grid=(b, l // tc),
        in_specs=[col(0), col(1), col(2), col(3), par, par, par, st,
                  _const_spec((tc, tc)), _const_spec((W_R, W_R))],
        out_specs=[pl.BlockSpec((1, tc, W_R), lambda i, c: (i, c, 0)), st],
        out_shape=[jax.ShapeDtypeStruct((b, l, W_R), F32),
                   jax.ShapeDtypeStruct((b, 2, LANES, LANES), F32)],
        scratch_shapes=[pltpu.VMEM((2, LANES, LANES), F32)],
        compiler_params=_cparams(("parallel", "arbitrary")),
        name="hgrn",
    )(bslab, bslab, bslab, bslab, log_lb.reshape(1, W_R), log1m_lb.reshape(1, W_R),
      gnorm.reshape(1, W_R), s0bd, blk_tri, _head_ones(W_R))


def _pair_state_in(s):
    b = s.shape[0]
    st = jnp.swapaxes(s, 2, 3).reshape(b, 2, 2, D_R, D_R)
    return jnp.einsum('bpevk,ef->bpevfk', st, jnp.eye(2, dtype=s.dtype)).reshape(b, 2, LANES, LANES)


def _pair_state_out(sbd):
    b = sbd.shape[0]
    s6 = sbd.reshape(b, 2, 2, D_R, 2, D_R)
    st = jnp.stack([s6[:, :, 0, :, 0, :], s6[:, :, 1, :, 1, :]], axis=2)
    return jnp.swapaxes(st.reshape(b, H_R, D_R, D_R), 2, 3)


RW_CHUNK = 64
RW_N = H_R * RW_CHUNK


RW_SEQS = 8


def _each(f, *lists):
    return [f(*args) for args in zip(*lists)]


def _rwkv_kernel(c_ref, sh0_ref, s0_ref, mu_ref, w0_ref, wup_ref, a0_ref, aup_ref, gup_ref,
                 kk_ref, ka_ref, rk_ref, lnw_ref, lnb_ref, tri_ref, ones_ref,
                 o_ref, sf_ref, prev_ref, s_ref, *, valid, passes):
    c = pl.program_id(1)

    @pl.when(c == 0)
    def _():
        prev_ref[...] = sh0_ref[...]
        s_ref[...] = s0_ref[...]

    seqs = list(range(RW_SEQS))
    cb = [c_ref[n] for n in seqs]
    tok = lax.broadcasted_iota(jnp.int32, (RW_CHUNK, C_PROJ), 0)
    shifted = [jnp.where(tok == 0, prev_ref[n], pltpu.roll(cb[n], 1, axis=0)) for n in seqs]
    for n in seqs:
        prev_ref[n] = cb[n][RW_CHUNK - 1:RW_CHUNK]
    xm = _each(lambda x, sh: x + (sh - x) * mu_ref[...], cb, shifted)
    r = [x[:, 0:W_R] for x in xm]
    k = [x[:, W_R:2 * W_R] for x in xm]
    v = [x[:, 2 * W_R:3 * W_R] for x in xm]
    tail = [x[:, 3 * W_R:C_PROJ] for x in xm]
    ones = ones_ref[...]

    w_log = _each(lambda t: -_softplus(-(w0_ref[...] + _mm(jnp.tanh(t), wup_ref[...]))) - 0.5, tail)
    lw = [-jnp.exp(x) for x in w_log]
    a = _each(lambda t: jax.nn.sigmoid(a0_ref[...] + _mm(t, aup_ref[...])), tail)
    g = _each(lambda t: _mm(jax.nn.sigmoid(t), gup_ref[...]), tail)
    kk = [x * kk_ref[...] for x in k]
    kk = _each(lambda x: x / jnp.maximum(jnp.sqrt(_head_sum(x * x, ones)), L2_EPS), kk)
    k = _each(lambda x, aa: x * (1.0 + (aa - 1.0) * ka_ref[...]), k, a)
    av = [-x for x in kk]
    bv = _each(lambda x, aa: x * aa, kk, a)
    if valid < RW_CHUNK:
        live = lax.broadcasted_iota(jnp.int32, (RW_CHUNK, W_R), 0) < valid
        zero = lambda xs: [jnp.where(live, x, 0.0) for x in xs]
        lw, av, bv, k, v = zero(lw), zero(av), zero(bv), zero(k), zero(v)

    def cumsum(x):
        hi, mid, lo = _split3(x)
        return _dot(tri_ref[...], hi) + (_dot(tri_ref[...], mid) + _dot(tri_ref[...], lo))

    cl = _each(cumsum, lw)
    last = [x[RW_CHUNK - 1:RW_CHUNK] for x in cl]
    e_neg = [jnp.exp(-x) for x in cl]
    e_end = _each(lambda l, x: jnp.exp(l - x), last, cl)

    lane_head = lax.broadcasted_iota(jnp.int32, (RW_CHUNK, W_R), 1) // D_R

    def stack(x):
        return jnp.concatenate([jnp.where(lane_head == h, x, 0.0) for h in range(H_R)], axis=0)

    m_a = _each(lambda x, c_, l_: stack(x * jnp.exp(c_ - l_)), av, cl, lw)
    m_b = _each(lambda x, e: stack(x * e), bv, e_neg)
    m_k = _each(lambda x, e: stack(x * e), k, e_neg)
    m_r = _each(lambda x, c_: stack(x * jnp.exp(c_)), r, cl)
    m_v = _each(stack, v)
    m_bh = _each(lambda x, e: stack(x * e), bv, e_end)
    m_kh = _each(lambda x, e: stack(x * e), k, e_end)

    row = lax.broadcasted_iota(jnp.int32, (RW_N, RW_N), 0)
    col = lax.broadcasted_iota(jnp.int32, (RW_N, RW_N), 1)
    mm = functools.partial(_mm, passes=passes)
    nt = functools.partial(mm, kind="nt")
    tn = functools.partial(mm, kind="tn")
    add = lambda x, y: x + y
    t_ab = _each(lambda x, y: jnp.where(row > col, nt(x, y), 0.0), m_a, m_b)
    t_ak = _each(lambda x, y: jnp.where(row > col, nt(x, y), 0.0), m_a, m_k)
    t_rb = _each(lambda x, y: jnp.where(row >= col, nt(x, y), 0.0), m_r, m_b)
    t_rk = _each(lambda x, y: jnp.where(row >= col, nt(x, y), 0.0), m_r, m_k)

    inv = [jnp.where(row == col, 1.0, t) for t in t_ab]
    pw = t_ab
    for _ in range(5):
        pw = _each(mm, pw, pw)
        inv = _each(add, inv, _each(mm, inv, pw))

    w1 = _each(mm, inv, m_a)
    w2 = _each(mm, inv, _each(mm, t_ak, m_v))
    rm = _each(add, m_r, _each(mm, t_rb, w1))
    yc = _each(add, _each(mm, t_rb, w2), _each(mm, t_rk, m_v))
    pm = _each(lambda l, x, y: jnp.where(row == col, jnp.exp(l), 0.0) + tn(x, y), last, w1, m_bh)
    qm = _each(add, _each(tn, w2, m_bh), _each(tn, m_v, m_kh))

    s = [s_ref[n] for n in seqs]
    y_st = _each(add, _each(nt, rm, s), yc)
    s_new = _each(lambda x, p_, q_: _mm(x, p_, passes=max(passes, 2)) + q_, s, pm, qm)
    for n in seqs:
        s_ref[n] = s_new[n]
    y = [(t[0:RW_CHUNK] + t[RW_CHUNK:2 * RW_CHUNK]) + (t[2 * RW_CHUNK:3 * RW_CHUNK] + t[3 * RW_CHUNK:])
         for t in y_st]

    mean = [_head_sum(x, ones) * (1.0 / D_R) for x in y]
    yc0 = _each(lambda x, m: x - m, y, mean)
    var = [_head_sum(x * x, ones) * (1.0 / D_R) for x in yc0]
    yn = _each(lambda x, vv: x * lax.rsqrt(vv + GN_EPS) * lnw_ref[...] + lnb_ref[...], yc0, var)
    bonus = _each(lambda rr, kk_, vv: _head_sum(rr * kk_ * rk_ref[...], ones) * vv, r, k, v)
    for n in seqs:
        o_ref[n] = (yn[n] + bonus[n]) * g[n]

    @pl.when(c == pl.num_programs(1) - 1)
    def _():
        sf_ref[...] = s_ref[...]


def _rwkv(cslab, shift0, s0bd, p, valid, passes=1):
    b, l, _ = cslab.shape
    row = lambda a: a.reshape(1, -1)
    pad_up = lambda w, off: jnp.zeros((LANES, W_R), F32).at[off:off + w.shape[0]].set(w).astype(BF16)
    t = np.arange(RW_CHUNK)[:, None]
    s = np.arange(RW_CHUNK)[None, :]
    consts = [row(p['rwkv_mu']), row(p['rwkv_w0']), pad_up(p['rwkv_w_up'], 0), row(p['rwkv_a0']),
              pad_up(p['rwkv_a_up'], 32), pad_up(p['rwkv_g_up'], 64), row(p['rwkv_k_k']),
              row(p['rwkv_k_a']), row(p['rwkv_r_k']), row(p['rwkv_ln_w']), row(p['rwkv_ln_b']),
              jnp.asarray((s <= t).astype(np.float32), dtype=BF16), _head_ones(W_R)]
    assert b % RW_SEQS == 0
    st = pl.BlockSpec((RW_SEQS, RW_N, RW_N), lambda i, c: (i, 0, 0))
    return pl.pallas_call(
        functools.partial(_rwkv_kernel, valid=valid, passes=passes),
        grid=(b // RW_SEQS, l // RW_CHUNK),
        in_specs=[pl.BlockSpec((RW_SEQS, RW_CHUNK, C_PROJ), lambda i, c: (i, c, 0)),
                  pl.BlockSpec((RW_SEQS, 1, C_PROJ), lambda i, c: (i, 0, 0)), st]
                 + [_const_spec(a.shape) for a in consts],
        out_specs=[pl.BlockSpec((RW_SEQS, RW_CHUNK, W_R), lambda i, c: (i, c, 0)), st],
        out_shape=[jax.ShapeDtypeStruct((b, l, W_R), F32),
                   jax.ShapeDtypeStruct((b, RW_N, RW_N), F32)],
        scratch_shapes=[pltpu.VMEM((RW_SEQS, 1, C_PROJ), F32), pltpu.VMEM((RW_SEQS, RW_N, RW_N), F32)],
        compiler_params=_cparams(("parallel", "arbitrary")),
        name="rwkv",
    )(cslab, shift0, s0bd, *consts)


def _quad_state_in(s):
    b = s.shape[0]
    return jnp.einsum('bhij,hg->bhigj', s, jnp.eye(H_R, dtype=s.dtype)).reshape(b, RW_N, RW_N)


def _quad_state_out(sbd):
    b = sbd.shape[0]
    s5 = sbd.reshape(b, H_R, D_R, H_R, D_R)
    return jnp.stack([s5[:, h, :, h, :] for h in range(H_R)], axis=1)


def _pad_tokens(a, multiple):
    pad = (-a.shape[1]) % multiple
    return a if pad == 0 else jnp.pad(a, ((0, 0), (0, pad), (0, 0)))


def _trunk_layer(x, b, l, sb_fn, s_hgrn0, s_rwkv0, shift0, mem_kv, lb, p, final_gain, prompt_kv=None,
                 valid=None):
    valid = l if valid is None else valid
    if prompt_kv is not None:
        layer, depth, kv_all = prompt_kv
        q, k, v, bslab, cslab = _in_proj_t(x, p['norm_mix'], p['w_in'], b, l, layer, depth, kv_all)
        o_a = sb_fn(q.reshape(b, l, W_A), k, v, layer=layer)
    else:
        q, k, v, bslab, cslab = _in_proj(x, p['norm_mix'], p['w_in'])
        real = lambda a: a.reshape(b, l, W_A)[:, :valid]
        o_a = _pad_tokens(sb_fn(real(q), real(k), real(v)), l)
        k, v = real(k).reshape(b, valid, H_A, D_A), real(v).reshape(b, valid, H_A, D_A)
    o_b, s_hgrn = _hgrn(_pad_tokens(bslab.reshape(b, l, 4 * W_R), HG_SUB), jnp.log(lb), jnp.log1p(-lb),
                        p['hgrn_norm'], _pair_state_in(s_hgrn0), valid)
    cslab = cslab.reshape(b, l, C_PROJ)
    o_c, s_rwkv = _rwkv(_pad_tokens(cslab, RW_CHUNK), shift0.reshape(b, 1, C_PROJ),
                        _quad_state_in(s_rwkv0), p, valid)
    x1, qm = _out_q(x, o_a.reshape(b * l, W_A), o_b[:, :l].reshape(b * l, W_R),
                    o_c[:, :l].reshape(b * l, W_R), p['w_out'], p['norm_mem'], p['mem_wq'])
    x2 = _mem_attn(x1.reshape(b, l, D_MODEL), qm.reshape(b, l, D_MODEL), *mem_kv, p['mem_wo'])
    x3 = _ffn(x2.reshape(b * l, D_MODEL), p['norm_ffn'], p['ffn_w_gate'], p['ffn_w_up'], p['ffn_w_down'],
              p['norm_ffn'] if final_gain is None else final_gain, final_gain is not None)
    return x3, k, v, _pair_state_out(s_hgrn), _quad_state_out(s_rwkv), cslab[:, valid - 1]


def kernel(x_prompt, x_sample, mem_prompt, cache_sb_k, cache_sb_v, page_table, state_hgrn, state_rwkv,
           state_rwkv_shift, cache_mem_k, cache_mem_v, norm_mix, w_in, sb_bias, hgrn_lb, hgrn_norm, rwkv_mu,
           rwkv_w0, rwkv_w_up, rwkv_a0, rwkv_a_up, rwkv_g_up, rwkv_k_k, rwkv_k_a, rwkv_r_k, rwkv_ln_w, rwkv_ln_b,
           w_out, norm_mem, mem_wq, mem_wk, mem_wv, mem_wo, norm_ffn, ffn_w_gate, ffn_w_up, ffn_w_down,
           final_norm):
    bp, lp, _ = x_prompt.shape
    bs, ls, _ = x_sample.shape
    depth = w_in.shape[0]
    n_pool = cache_sb_k.shape[1]
    lb_all = jnp.cumsum(jax.nn.softmax(hgrn_lb.astype(F32), axis=0), axis=0)
    lb_all = lb_all - lb_all[0]
    assert page_table.shape[1] % SB_PAGES == 0 and x_sample.shape[1] == N_DEC
    cache_kt = jnp.transpose(cache_sb_k, (0, 1, 3, 4, 2))
    cache_vt = jnp.transpose(cache_sb_v, (0, 1, 3, 4, 2))
    mem_tok = mem_prompt.reshape(bp * N_MEM, D_MODEL)
    mem_k_s = cache_mem_k.reshape(depth, bs, N_MEM, D_MODEL)
    mem_v_s = cache_mem_v.reshape(depth, bs, N_MEM, D_MODEL)
    xp = x_prompt.reshape(bp * lp, D_MODEL)
    ls_pad = ls + (-ls) % SUBLANES
    xs = _pad_tokens(x_sample, SUBLANES).reshape(bs * ls_pad, D_MODEL)
    outs = [[] for _ in range(10)]
    kv_p = None
    for l in range(depth):
        bf = lambda w: w[l].astype(BF16)
        p = dict(norm_mix=norm_mix[l], w_in=bf(w_in), hgrn_norm=hgrn_norm[l], rwkv_mu=rwkv_mu[l],
                 rwkv_w0=rwkv_w0[l], rwkv_w_up=rwkv_w_up[l], rwkv_a0=rwkv_a0[l], rwkv_a_up=rwkv_a_up[l],
                 rwkv_g_up=rwkv_g_up[l], rwkv_k_k=rwkv_k_k[l], rwkv_k_a=rwkv_k_a[l], rwkv_r_k=rwkv_r_k[l],
                 rwkv_ln_w=rwkv_ln_w[l], rwkv_ln_b=rwkv_ln_b[l], w_out=bf(w_out), norm_mem=norm_mem[l],
                 mem_wq=bf(mem_wq), mem_wo=bf(mem_wo), norm_ffn=norm_ffn[l], ffn_w_gate=bf(ffn_w_gate),
                 ffn_w_up=bf(ffn_w_up), ffn_w_down=bf(ffn_w_down))
        gain = final_norm if l == depth - 1 else None
        bias = sb_bias[l]
        mk_p = _matmul(mem_tok, bf(mem_wk)).reshape(bp, N_MEM, D_MODEL)
        mv_p = _matmul(mem_tok, bf(mem_wv)).reshape(bp, N_MEM, D_MODEL)
        xp, kt_p, vt_p, hg_p, rw_p, sh_p = _trunk_layer(
            xp, bp, lp, functools.partial(_sb_prompt, bias=bias),
            jnp.zeros((bp, H_R, D_R, D_R), F32), jnp.zeros((bp, H_R, D_R, D_R), F32),
            jnp.zeros((bp, C_PROJ), F32), (mk_p[None], mv_p[None], 0), lb_all[l], p, gain, (l, depth, kv_p))
        kv_p = (kt_p, vt_p)
        sb_fn = functools.partial(_sb_sample, bias=bias, cache_kt=cache_kt, cache_vt=cache_vt, layer=l,
                                  page_table=page_table)
        xs, ks, vs, hg_s, rw_s, sh_s = _trunk_layer(
            xs, bs, ls_pad, sb_fn, state_hgrn[l], state_rwkv[l], state_rwkv_shift[l],
            (mem_k_s, mem_v_s, l), lb_all[l], p, gain, valid=ls)
        for lst, val in zip(outs, (ks, vs, hg_p, hg_s, rw_p, rw_s, sh_p, sh_s,
                                   mk_p.reshape(bp, N_MEM, H_M, D_M), mv_p.reshape(bp, N_MEM, H_M, D_M))):
            lst.append(val)
    heads_last = lambda a: jnp.transpose(a.reshape(depth, bp, H_A, D_A, lp), (0, 1, 4, 2, 3))
    return ((xp.reshape(bp, lp, D_MODEL), xs.reshape(bs, ls_pad, D_MODEL)[:, :ls],
             heads_last(kv_p[0]), heads_last(kv_p[1]))
            + tuple(jnp.stack(o) for o in outs))
```

```python
import functools

import jax
import jax.numpy as jnp
from jax import lax
from jax.experimental import pallas as pl
from jax.experimental.pallas import tpu as pltpu
import numpy as np

F32 = jnp.float32
BF16 = jnp.bfloat16

D_MODEL = 1024
H_A, D_A = 8, 64
W_A = H_A * D_A
H_R, D_R = 4, 64
W_R = H_R * D_R
H_M, D_M = 4, 256
N_MEM = 256
C_PROJ = 3 * W_R + 128
D_IN = 3 * W_A + 4 * W_R + C_PROJ
PAGE = 128
RMS_EPS = 1e-6
GN_EPS = 64e-5
L2_EPS = 1e-12

LANES = 128
SUBLANES = 8
VMEM_LIMIT = 56 * 1024 * 1024

NEG_BIG = -1e30


def _cparams(sem):
    return pltpu.CompilerParams(dimension_semantics=sem, vmem_limit_bytes=VMEM_LIMIT)


def _dot(a, b):
    return jnp.dot(a, b, preferred_element_type=F32)


def _dot_nt(a, b):
    return lax.dot_general(a, b, (((1,), (1,)), ((), ())), preferred_element_type=F32)


def _dot_tn(a, b):
    return lax.dot_general(a, b, (((0,), (0,)), ((), ())), preferred_element_type=F32)


def _split2(x):
    hi = x.astype(BF16)
    lo = (x - hi.astype(F32)).astype(BF16)
    return hi, lo


def _split3(x):
    hi = x.astype(BF16)
    r = x - hi.astype(F32)
    mid = r.astype(BF16)
    lo = (r - mid.astype(F32)).astype(BF16)
    return hi, mid, lo


def _rms(x, g):
    ms = jnp.mean(x * x, axis=-1, keepdims=True)
    return (x * lax.rsqrt(ms + RMS_EPS)) * g


def _log1p_exp(x):
    return jnp.log(1.0 + jnp.exp(-jnp.abs(x)))


def _const_spec(shape):
    nd = len(shape)
    return pl.BlockSpec(shape, lambda *_: (0,) * nd)


def _in_proj_kernel(x_ref, g_ref, w_ref, q_ref, k_ref, v_ref, b_ref, c_ref):
    hb = _rms(x_ref[...], g_ref[...]).astype(BF16)
    q_ref[...] = _dot(hb, w_ref[:, 0:W_A])
    k_ref[...] = _dot(hb, w_ref[:, W_A:2 * W_A])
    v_ref[...] = _dot(hb, w_ref[:, 2 * W_A:3 * W_A])
    b_ref[...] = _dot(hb, w_ref[:, 3 * W_A:3 * W_A + 4 * W_R])
    c_ref[...] = _dot(hb, w_ref[:, 3 * W_A + 4 * W_R:D_IN])


def _in_proj(x, g, w_bf):
    m = x.shape[0]
    tm = min(m, 512)
    widths = (W_A, W_A, W_A, 4 * W_R, C_PROJ)
    return pl.pallas_call(
        _in_proj_kernel,
        grid=(m // tm,),
        in_specs=[pl.BlockSpec((tm, D_MODEL), lambda i: (i, 0)),
                  _const_spec((1, D_MODEL)),
                  _const_spec((D_MODEL, D_IN))],
        out_specs=[pl.BlockSpec((tm, w), lambda i: (i, 0)) for w in widths],
        out_shape=[jax.ShapeDtypeStruct((m, w), F32) for w in widths],
        compiler_params=_cparams(("parallel",)),
        name="in_proj",
    )(x, g.reshape(1, D_MODEL), w_bf)


def _in_proj_t_kernel(x_ref, g_ref, w_ref, wkvt_ref, q_ref, kt_ref, vt_ref, b_ref, c_ref):
    hb = _rms(x_ref[...], g_ref[...]).astype(BF16)
    q_ref[...] = _dot(hb, w_ref[:, 0:W_A])
    kt_ref[0] = _dot_nt(wkvt_ref[0:W_A, :], hb)
    vt_ref[0] = _dot_nt(wkvt_ref[W_A:2 * W_A, :], hb)
    b_ref[...] = _dot(hb, w_ref[:, 3 * W_A:3 * W_A + 4 * W_R])
    c_ref[...] = _dot(hb, w_ref[:, 3 * W_A + 4 * W_R:D_IN])


def _in_proj_t(x, g, w_bf, b, l):
    m = x.shape[0]
    tm = min(l, 512)
    assert l % tm == 0
    per_seq = l // tm
    wkvt = w_bf[:, W_A:3 * W_A].T
    row = lambda w: pl.BlockSpec((tm, w), lambda i: (i, 0))
    tr = pl.BlockSpec((1, W_A, tm), lambda i: (i // per_seq, 0, i % per_seq))
    return pl.pallas_call(
        _in_proj_t_kernel,
        grid=(m // tm,),
        in_specs=[row(D_MODEL), _const_spec((1, D_MODEL)), _const_spec((D_MODEL, D_IN)),
                  _const_spec((2 * W_A, D_MODEL))],
        out_specs=[row(W_A), tr, tr, row(4 * W_R), row(C_PROJ)],
        out_shape=[jax.ShapeDtypeStruct((m, W_A), F32),
                   jax.ShapeDtypeStruct((b, W_A, l), F32),
                   jax.ShapeDtypeStruct((b, W_A, l), F32),
                   jax.ShapeDtypeStruct((m, 4 * W_R), F32),
                   jax.ShapeDtypeStruct((m, C_PROJ), F32)],
        compiler_params=_cparams(("parallel",)),
        name="in_proj_t",
    )(x, g.reshape(1, D_MODEL), w_bf, wkvt)


def _matmul_kernel(x_ref, w_ref, o_ref):
    o_ref[...] = _dot(x_ref[...].astype(BF16), w_ref[...])


def _matmul(x, w_bf):
    m, k = x.shape
    n = w_bf.shape[1]
    tm = min(m, 512)
    return pl.pallas_call(
        _matmul_kernel,
        grid=(m // tm,),
        in_specs=[pl.BlockSpec((tm, k), lambda i: (i, 0)), _const_spec((k, n))],
        out_specs=pl.BlockSpec((tm, n), lambda i: (i, 0)),
        out_shape=jax.ShapeDtypeStruct((m, n), F32),
        compiler_params=_cparams(("parallel",)),
        name="matmul",
    )(x, w_bf)


def _out_q_kernel(x_ref, oa_ref, ob_ref, oc_ref, wo_ref, g_ref, wq_ref, x1_ref, qm_ref):
    x1 = (x_ref[...]
          + _dot(oa_ref[...].astype(BF16), wo_ref[0:W_A, :])
          + _dot(ob_ref[...].astype(BF16), wo_ref[W_A:W_A + W_R, :])
          + _dot(oc_ref[...].astype(BF16), wo_ref[W_A + W_R:W_A + 2 * W_R, :]))
    x1_ref[...] = x1
    qm_ref[...] = _dot(_rms(x1, g_ref[...]).astype(BF16), wq_ref[...])


def _out_q(x, oa, ob, oc, wo_bf, g, wq_bf):
    m = x.shape[0]
    tm = min(m, 512)
    row = lambda w: pl.BlockSpec((tm, w), lambda i: (i, 0))
    return pl.pallas_call(
        _out_q_kernel,
        grid=(m // tm,),
        in_specs=[row(D_MODEL), row(W_A), row(W_R), row(W_R),
                  _const_spec((D_MODEL, D_MODEL)), _const_spec((1, D_MODEL)),
                  _const_spec((D_MODEL, D_MODEL))],
        out_specs=[row(D_MODEL), row(D_MODEL)],
        out_shape=[jax.ShapeDtypeStruct((m, D_MODEL), F32)] * 2,
        compiler_params=_cparams(("parallel",)),
        name="out_q",
    )(x, oa, ob, oc, wo_bf, g.reshape(1, D_MODEL), wq_bf)


def _mem_attn_kernel(x_ref, q_ref, mk_ref, mv_ref, wo_ref, o_ref):
    sls = [slice(h * D_M, (h + 1) * D_M) for h in range(H_M)]
    qs = [(q_ref[0, :, sl] * (D_M ** -0.5)).astype(BF16) for sl in sls]
    ss = [_dot_nt(q, mk_ref[0, 0, :, sl].astype(BF16)) for q, sl in zip(qs, sls)]
    es = [jnp.exp(s - jnp.max(s, axis=-1, keepdims=True)) for s in ss]
    ps = [(e / jnp.sum(e, axis=-1, keepdims=True)).astype(BF16) for e in es]
    ohs = [_dot(p, mv_ref[0, 0, :, sl].astype(BF16)).astype(BF16) for p, sl in zip(ps, sls)]
    acc = x_ref[0]
    for oh, sl in zip(ohs, sls):
        acc = acc + _dot(oh, wo_ref[sl, :])
    o_ref[0] = acc


def _mem_attn(x1, qm, mk, mv, group, wo_bf):
    b, l, _ = x1.shape
    tl = min(l, 512)
    tok = pl.BlockSpec((1, tl, D_MODEL), lambda i, j: (i, j, 0))
    mem = pl.BlockSpec((1, 1, N_MEM, D_MODEL), lambda i, j: (group, i, 0, 0))
    return pl.pallas_call(
        _mem_attn_kernel,
        grid=(b, l // tl),
        in_specs=[tok, tok, mem, mem, _const_spec((D_MODEL, D_MODEL))],
        out_specs=tok,
        out_shape=jax.ShapeDtypeStruct((b, l, D_MODEL), F32),
        compiler_params=_cparams(("parallel", "parallel")),
        name="mem_attn",
    )(x1, qm, mk, mv, wo_bf)


FF_CHUNK = 256


def _ffn_kernel(x_ref, g_ref, wg_ref, wu_ref, wd_ref, gf_ref, o_ref, acc_ref, *, final_norm):
    x = x_ref[...]
    hb = _rms(x, g_ref[...]).astype(BF16)
    acc_ref[...] = x
    d_ff = wg_ref.shape[1]
    for c in range(d_ff // FF_CHUNK):
        sl = slice(c * FF_CHUNK, (c + 1) * FF_CHUNK)
        gate = _dot(hb, wg_ref[:, sl])
        up = _dot(hb, wu_ref[:, sl])
        a = (gate * jax.nn.sigmoid(gate)) * up
        acc_ref[...] += _dot(a.astype(BF16), wd_ref[sl, :])
    if final_norm:
        o_ref[...] = _rms(acc_ref[...], gf_ref[...])
    else:
        o_ref[...] = acc_ref[...]


def _ffn(x, g, wg_bf, wu_bf, wd_bf, gf, final_norm):
    m = x.shape[0]
    d_ff = wg_bf.shape[1]
    tm = min(m, 512)
    row = pl.BlockSpec((tm, D_MODEL), lambda i: (i, 0))
    return pl.pallas_call(
        functools.partial(_ffn_kernel, final_norm=final_norm),
        grid=(m // tm,),
        in_specs=[row, _const_spec((1, D_MODEL)), _const_spec((D_MODEL, d_ff)),
                  _const_spec((D_MODEL, d_ff)), _const_spec((d_ff, D_MODEL)),
                  _const_spec((1, D_MODEL))],
        out_specs=row,
        out_shape=jax.ShapeDtypeStruct((m, D_MODEL), F32),
        scratch_shapes=[pltpu.VMEM((tm, D_MODEL), F32)],
        compiler_params=_cparams(("parallel",)),
        name="ffn",
    )(x, g.reshape(1, D_MODEL), wg_bf, wu_bf, wd_bf, gf.reshape(1, D_MODEL))


def _log_terms(z):
    nln = jnp.maximum(z, 0.0) + _log1p_exp(z)
    return nln, z - nln


def _split_trunc(x):
    hi = lax.bitcast_convert_type(lax.bitcast_convert_type(x, jnp.int32) & jnp.int32(-65536), F32)
    return hi.astype(BF16), (x - hi).astype(BF16)


SB_GROUP = 4


def _sb_prompt_kernel(bias_ref, q_ref, kt_ref, vt_ref, u_ref, o_ref, qs_ref, kb_ref, vs_ref, carry_ref,
                      ls_ref, hl_ref, w_ref):
    i = pl.program_id(1)
    nkb = kt_ref.shape[3] // PAGE
    lane = lax.broadcasted_iota(jnp.int32, (PAGE, LANES), 1)
    row = lax.broadcasted_iota(jnp.int32, (PAGE, LANES), 0)
    even = lane < D_A
    even_row = row < D_A
    causal = lane < row

    @pl.when(i == 0)
    def _():
        for j in range(nkb):
            cols = slice(j * PAGE, (j + 1) * PAGE)
            kb_ref[j] = kt_ref[0, 0, :, cols].astype(BF16)
            for p in range(H_A // 2):
                v2 = vt_ref[0, 0, p * LANES:(p + 1) * LANES, cols]
                vs_ref[p, j, :, 0:PAGE] = jnp.where(even_row, v2, 0.0).astype(BF16)
                vs_ref[p, j, :, PAGE:2 * PAGE] = jnp.where(even_row, 0.0, v2).astype(BF16)

    for p in range(H_A // 2):
        q2 = q_ref[0, :, p * LANES:(p + 1) * LANES] * (D_A ** -0.5)
        qs_ref[p, 0:PAGE, :] = jnp.where(even, q2, 0.0).astype(BF16)
        qs_ref[p, PAGE:2 * PAGE, :] = jnp.where(even, 0.0, q2).astype(BF16)

    def blocks(js, diag):
        for n, j in enumerate(js):
            for p in range(H_A // 2):
                s2 = _dot(qs_ref[p], kb_ref[j, p * LANES:(p + 1) * LANES, :])
                for e in range(2):
                    h = 2 * p + e
                    z = s2[e * PAGE:(e + 1) * PAGE] + bias_ref[h]
                    nln, ls = _log_terms(z)
                    if diag and n == 0:
                        nln = jnp.where(causal, nln, 0.0)
                    hi, lo = _split_trunc(nln)
                    ls_ref[n, h] = ls
                    hl_ref[n, h, :, 0:PAGE] = hi
                    hl_ref[n, h, :, PAGE:2 * PAGE] = lo
        for n, j in enumerate(js):
            for h in range(H_A):
                r = _dot(hl_ref[n, h], u_ref[...])
                between, tot = r[:, 0:LANES], r[:, LANES:2 * LANES]
                if diag and n == 0:
                    w = jnp.where(causal, jnp.exp(ls_ref[n, h] + between), 0.0)
                    carry_ref[h] = tot
                else:
                    c = carry_ref[h]
                    w = jnp.exp(ls_ref[n, h] + between + c)
                    carry_ref[h] = c + tot
                w_ref[n, h // 2, :, (h % 2) * PAGE:(h % 2 + 1) * PAGE] = w.astype(BF16)
        for p in range(H_A // 2):
            pv = _dot_nt(w_ref[0, p], vs_ref[p, js[0]])
            for n in range(1, len(js)):
                pv = pv + _dot_nt(w_ref[n, p], vs_ref[p, js[n]])
            if diag:
                o_ref[0, :, p * LANES:(p + 1) * LANES] = pv
            else:
                o_ref[0, :, p * LANES:(p + 1) * LANES] += pv

    blocks([i], True)
    rem = i % SB_GROUP
    for r in range(1, SB_GROUP):
        @pl.when(rem == r)
        def _(r=r):
            blocks([i - 1 - n for n in range(r)], False)

    def older(t, _):
        j = i - 1 - rem - SB_GROUP * t
        blocks([j - n for n in range(SB_GROUP)], False)
        return 0

    lax.fori_loop(0, i // SB_GROUP, older, 0)


def _sb_tri():
    j = np.arange(2 * PAGE)[:, None] % PAGE
    s = np.arange(2 * LANES)[None, :]
    return jnp.asarray(-((j > s) | (s >= LANES)).astype(np.float32), dtype=BF16)


def _sb_prompt(q, kt, vt, bias, layer=0):
    b, l, _ = q.shape
    nq = l // PAGE
    tok = pl.BlockSpec((1, PAGE, W_A), lambda i, j: (i, j, 0))
    seq = pl.BlockSpec((1, 1, W_A, l), lambda i, j: (layer, i, 0, 0))
    return pl.pallas_call(
        _sb_prompt_kernel,
        grid=(b, nq),
        in_specs=[pl.BlockSpec(memory_space=pltpu.SMEM), tok, seq, seq,
                  _const_spec((2 * PAGE, 2 * LANES))],
        out_specs=tok,
        out_shape=jax.ShapeDtypeStruct((b, l, W_A), F32),
        scratch_shapes=[pltpu.VMEM((H_A // 2, 2 * PAGE, LANES), BF16),
                        pltpu.VMEM((nq, W_A, PAGE), BF16),
                        pltpu.VMEM((H_A // 2, nq, LANES, 2 * PAGE), BF16),
                        pltpu.VMEM((H_A, PAGE, LANES), F32),
                        pltpu.VMEM((SB_GROUP, H_A, PAGE, LANES), F32),
                        pltpu.VMEM((SB_GROUP, H_A, PAGE, 2 * PAGE), BF16),
                        pltpu.VMEM((SB_GROUP, H_A // 2, PAGE, 2 * PAGE), BF16)],
        compiler_params=_cparams(("parallel", "arbitrary")),
        name="sb_prompt",
    )(bias, q, kt, vt, _sb_tri())


N_DEC = 4


SB_PAGES = 16
N_QROW = N_DEC * H_A


def _sb_sample_kernel(pt_ref, q_ref, br_ref, kn_ref, vn_ref, *rest):
    del pt_ref
    kp_refs, vp_refs = rest[:SB_PAGES], rest[SB_PAGES:2 * SB_PAGES]
    u_ref, o_ref, acc_ref, carry_ref = rest[2 * SB_PAGES:]
    j = pl.program_id(1)

    def logits(kt):
        z = _dot(q_ref[0], kt.astype(BF16)) + br_ref[...]
        return _log_terms(z)

    def suffix(nln):
        hi, lo = _split_trunc(nln)
        r = _dot(jnp.concatenate([hi, lo], axis=1), u_ref[...])
        return r[:, 0:LANES], r[:, LANES:2 * LANES]

    @pl.when(j == 0)
    def _():
        key = lax.broadcasted_iota(jnp.int32, (N_QROW, LANES), 1)
        tok = lax.broadcasted_iota(jnp.int32, (N_QROW, LANES), 0) // H_A
        visible = key < tok
        ln, ls = logits(kn_ref[0])
        ln = jnp.where(visible, ln, 0.0)
        between, tot = suffix(ln)
        w = jnp.where(visible, jnp.exp(ls + between), 0.0)
        carry_ref[...] = tot
        acc_ref[...] = _dot_nt(w.astype(BF16), vn_ref[0].astype(BF16))

    @pl.when(j > 0)
    def _():
        terms = [logits(kp_refs[i][0, 0].reshape(W_A, PAGE)) for i in range(SB_PAGES)]
        sums = [suffix(ln) for ln, _ in terms]
        c = carry_ref[...]
        ws = []
        for (_, ls), (between, tot) in zip(terms, sums):
            ws.append(jnp.exp(ls + between + c).astype(BF16))
            c = c + tot
        carry_ref[...] = c
        acc = acc_ref[...]
        for i in range(SB_PAGES):
            acc = acc + _dot_nt(ws[i], vp_refs[i][0, 0].reshape(W_A, PAGE).astype(BF16))
        acc_ref[...] = acc

    @pl.when(j == pl.num_programs(1) - 1)
    def _():
        head = lax.broadcasted_iota(jnp.int32, (H_A, W_A), 0)
        lane_head = lax.broadcasted_iota(jnp.int32, (H_A, W_A), 1) // D_A
        for t in range(N_DEC):
            rows = acc_ref[t * H_A:(t + 1) * H_A, :]
            o_ref[0, t:t + 1, :] = jnp.sum(jnp.where(head == lane_head, rows, 0.0), axis=0, keepdims=True)


def _sb_sample(q, k, v, bias, cache_kt, cache_vt, layer, page_table):
    b = q.shape[0]
    n_pages = page_table.shape[1]
    eye = jnp.eye(H_A, dtype=F32)
    q4 = q.reshape(b, N_DEC, H_A, D_A) * (D_A ** -0.5)
    qbd = jnp.einsum('bthd,hg->btghd', q4, eye).reshape(b, N_QROW, W_A).astype(BF16)
    bias_rows = jnp.broadcast_to(jnp.tile(bias, N_DEC)[:, None], (N_QROW, LANES))
    new_t = lambda a: jnp.pad(jnp.swapaxes(a, 1, 2), ((0, 0), (0, 0), (0, PAGE - N_DEC)))

    def page_map(slot):
        def index(i, j, pt):
            return (layer, pt[i, n_pages - 1 - ((jnp.maximum(j, 1) - 1) * SB_PAGES + slot)], 0, 0, 0)
        return pl.BlockSpec((1, 1, H_A, D_A, PAGE), index)

    per_b = lambda shape: pl.BlockSpec((1,) + shape, lambda i, j, pt: (i, 0, 0))
    pages = [page_map(s) for s in range(SB_PAGES)]
    grid_spec = pltpu.PrefetchScalarGridSpec(
        num_scalar_prefetch=1,
        grid=(b, n_pages // SB_PAGES + 1),
        in_specs=[per_b((N_QROW, W_A)),
                  pl.BlockSpec((N_QROW, LANES), lambda i, j, pt: (0, 0)),
                  per_b((W_A, PAGE)), per_b((W_A, PAGE))] + pages + pages
                 + [pl.BlockSpec((2 * PAGE, 2 * LANES), lambda i, j, pt: (0, 0))],
        out_specs=per_b((N_DEC, W_A)),
        scratch_shapes=[pltpu.VMEM((N_QROW, W_A), F32), pltpu.VMEM((N_QROW, LANES), F32)],
    )
    return pl.pallas_call(
        _sb_sample_kernel,
        grid_spec=grid_spec,
        out_shape=jax.ShapeDtypeStruct((b, N_DEC, W_A), F32),
        compiler_params=_cparams(("parallel", "arbitrary")),
        name="sb_sample",
    )(page_table, qbd, bias_rows, new_t(k), new_t(v), *([cache_kt] * SB_PAGES), *([cache_vt] * SB_PAGES),
      _sb_tri())


def _mm(a, b, kind="nn", passes=1):
    dot = {"nn": _dot, "nt": _dot_nt, "tn": _dot_tn}[kind]
    if passes == 1:
        return dot(a.astype(BF16), b.astype(BF16))
    a_hi, a_lo = _split2(a)
    if passes == 2:
        b_hi = b.astype(BF16)
        return dot(a_hi, b_hi) + dot(a_lo, b_hi)
    b_hi, b_lo = _split2(b)
    return dot(a_hi, b_hi) + (dot(a_hi, b_lo) + dot(a_lo, b_hi))


def _head_ones(n):
    r = np.arange(n)[:, None] // D_R
    c = np.arange(n)[None, :] // D_R
    return jnp.asarray((r == c).astype(np.float32), dtype=BF16)


def _head_sum(x, ones):
    hi, mid, lo = _split3(x)
    return _dot(hi, ones) + (_dot(mid, ones) + _dot(lo, ones))


def _softplus(x):
    return jnp.maximum(x, 0.0) + _log1p_exp(x)


HG_SUB = 16


def _hgrn_kernel(q_ref, f_ref, i_ref, g_ref, la_ref, l1_ref, gn_ref, s0_ref, bt_ref, ones_ref,
                 o_ref, sf_ref, s_ref, *, valid):
    c = pl.program_id(1)

    @pl.when(c == 0)
    def _():
        s_ref[...] = s0_ref[0]

    tc = q_ref.shape[1]
    bq = q_ref[0]
    qs = bq * jax.nn.sigmoid(bq)
    bf = f_ref[0]
    la = la_ref[...]
    l1p = _log1p_exp(bf)
    lb = l1_ref[...] + (jnp.minimum(bf, 0.0) - l1p)
    lf = jnp.maximum(la, lb) + _log1p_exp(la - lb)
    kk = jnp.exp(l1_ref[...] - (jnp.maximum(bf, 0.0) + l1p))
    if valid < tc:
        live = lax.broadcasted_iota(jnp.int32, (tc, W_R), 0) < valid
        lf = jnp.where(live, lf, 0.0)
        kk = jnp.where(live, kk, 0.0)
    hi, mid, lo = _split3(lf)
    cum = _dot(bt_ref[...], hi) + (_dot(bt_ref[...], mid) + _dot(bt_ref[...], lo))
    iv = i_ref[0]

    trow = lax.broadcasted_iota(jnp.int32, (HG_SUB, LANES), 0)
    r2 = lax.broadcasted_iota(jnp.int32, (LANES, LANES), 0) // D_R
    c2 = lax.broadcasted_iota(jnp.int32, (LANES, LANES), 1) // D_R
    same_head = r2 == c2
    ones_pair = ones_ref[0:LANES, 0:LANES]

    n_sub = tc // HG_SUB
    units = [(sc, p) for sc in range(n_sub) for p in range(2)]
    part = lambda x, sc, p: x[sc * HG_SUB:(sc + 1) * HG_SUB, p * LANES:(p + 1) * LANES]

    scores, upd, q_dec, decay, ivs = {}, {}, {}, {}, {}
    for u in units:
        cumc, qc, kc, ic = (part(x, *u) for x in (cum, qs, kk, iv))
        last = cumc[HG_SUB - 1:HG_SUB]
        ps = []
        for j in range(HG_SUB):
            d = jnp.where(trow >= j, cumc - cumc[j:j + 1], NEG_BIG)
            ps.append(jnp.exp(d) * qc * kc[j:j + 1])
        scores[u] = _dot(jnp.concatenate(ps, axis=0).astype(BF16), ones_pair)
        upd[u] = _dot_tn(ic.astype(BF16), (kc * jnp.exp(last - cumc)).astype(BF16))
        q_dec[u] = (qc * jnp.exp(cumc)).astype(BF16)
        decay[u] = jnp.exp(last)
        ivs[u] = ic

    s = [s_ref[0], s_ref[1]]
    inter = {}
    for u in units:
        p = u[1]
        inter[u] = _dot_nt(q_dec[u], s[p].astype(BF16))
        s[p] = s[p] * decay[u] + jnp.where(same_head, upd[u], 0.0)
    s_ref[0] = s[0]
    s_ref[1] = s[1]

    outs = {}
    for u in units:
        intra = scores[u][0:HG_SUB] * ivs[u][0:1]
        for j in range(1, HG_SUB):
            intra = intra + scores[u][j * HG_SUB:(j + 1) * HG_SUB] * ivs[u][j:j + 1]
        outs[u] = inter[u] + intra
    o = jnp.concatenate([jnp.concatenate([outs[(sc, p)] for p in range(2)], axis=1) for sc in range(n_sub)],
                        axis=0)
    ms = _head_sum(o * o, ones_ref[...]) * (1.0 / D_R)
    bg = g_ref[0]
    o_ref[0] = (o * lax.rsqrt(ms + RMS_EPS)) * gn_ref[...] * (bg * jax.nn.sigmoid(bg))

    @pl.when(c == pl.num_programs(1) - 1)
    def _():
        sf_ref[0] = s_ref[...]


def _hgrn(bslab, log_lb, log1m_lb, gnorm, s0bd, valid):
    b, l, _ = bslab.shape
    tc = min(l, 256)
    col = lambda k: pl.BlockSpec((1, tc, W_R), lambda i, c: (i, c, k))
    par = _const_spec((1, W_R))
    st = pl.BlockSpec((1, 2, LANES, LANES), lambda i, c: (i, 0, 0, 0))
    t = np.arange(tc)[:, None]
    s = np.arange(tc)[None, :]
    blk_tri = jnp.asarray(((t // HG_SUB == s // HG_SUB) & (s <= t)).astype(np.float32), dtype=BF16)
    return pl.pallas_call(
        functools.partial(_hgrn_kernel, valid=valid),
        grid=(b, l // tc),
        in_specs=[col(0), col(1), col(2), col(3), par, par, par, st,
                  _const_spec((tc, tc)), _const_spec((W_R, W_R))],
        out_specs=[pl.BlockSpec((1, tc, W_R), lambda i, c: (i, c, 0)), st],
        out_shape=[jax.ShapeDtypeStruct((b, l, W_R), F32),
                   jax.ShapeDtypeStruct((b, 2, LANES, LANES), F32)],
        scratch_shapes=[pltpu.VMEM((2, LANES, LANES), F32)],
        compiler_params=_cparams(("parallel", "arbitrary")),
        name="hgrn",
    )(bslab, bslab, bslab, bslab, log_lb.reshape(1, W_R), log1m_lb.reshape(1, W_R),
      gnorm.reshape(1, W_R), s0bd, blk_tri, _head_ones(W_R))


def _pair_state_in(s):
    b = s.shape[0]
    st = jnp.swapaxes(s, 2, 3).reshape(b, 2, 2, D_R, D_R)
    return jnp.einsum('bpevk,ef->bpevfk', st, jnp.eye(2, dtype=s.dtype)).reshape(b, 2, LANES, LANES)


def _pair_state_out(sbd):
    b = sbd.shape[0]
    s6 = sbd.reshape(b, 2, 2, D_R, 2, D_R)
    st = jnp.stack([s6[:, :, 0, :, 0, :], s6[:, :, 1, :, 1, :]], axis=2)
    return jnp.swapaxes(st.reshape(b, H_R, D_R, D_R), 2, 3)


RW_CHUNK = 64
RW_N = H_R * RW_CHUNK


RW_SEQS = 8


def _each(f, *lists):
    return [f(*args) for args in zip(*lists)]


def _rwkv_kernel(c_ref, sh0_ref, s0_ref, mu_ref, w0_ref, wup_ref, a0_ref, aup_ref, gup_ref,
                 kk_ref, ka_ref, rk_ref, lnw_ref, lnb_ref, tri_ref, ones_ref,
                 o_ref, sf_ref, prev_ref, s_ref, *, valid, passes):
    c = pl.program_id(1)

    @pl.when(c == 0)
    def _():
        prev_ref[...] = sh0_ref[...]
        s_ref[...] = s0_ref[...]

    seqs = list(range(RW_SEQS))
    cb = [c_ref[n] for n in seqs]
    tok = lax.broadcasted_iota(jnp.int32, (RW_CHUNK, C_PROJ), 0)
    shifted = [jnp.where(tok == 0, prev_ref[n], pltpu.roll(cb[n], 1, axis=0)) for n in seqs]
    for n in seqs:
        prev_ref[n] = cb[n][RW_CHUNK - 1:RW_CHUNK]
    xm = _each(lambda x, sh: x + (sh - x) * mu_ref[...], cb, shifted)
    r = [x[:, 0:W_R] for x in xm]
    k = [x[:, W_R:2 * W_R] for x in xm]
    v = [x[:, 2 * W_R:3 * W_R] for x in xm]
    tail = [x[:, 3 * W_R:C_PROJ] for x in xm]
    ones = ones_ref[...]

    w_log = _each(lambda t: -_softplus(-(w0_ref[...] + _mm(jnp.tanh(t), wup_ref[...]))) - 0.5, tail)
    lw = [-jnp.exp(x) for x in w_log]
    a = _each(lambda t: jax.nn.sigmoid(a0_ref[...] + _mm(t, aup_ref[...])), tail)
    g = _each(lambda t: _mm(jax.nn.sigmoid(t), gup_ref[...]), tail)
    kk = [x * kk_ref[...] for x in k]
    kk = _each(lambda x: x / jnp.maximum(jnp.sqrt(_head_sum(x * x, ones)), L2_EPS), kk)
    k = _each(lambda x, aa: x * (1.0 + (aa - 1.0) * ka_ref[...]), k, a)
    av = [-x for x in kk]
    bv = _each(lambda x, aa: x * aa, kk, a)
    if valid < RW_CHUNK:
        live = lax.broadcasted_iota(jnp.int32, (RW_CHUNK, W_R), 0) < valid
        zero = lambda xs: [jnp.where(live, x, 0.0) for x in xs]
        lw, av, bv, k, v = zero(lw), zero(av), zero(bv), zero(k), zero(v)

    def cumsum(x):
        hi, mid, lo = _split3(x)
        return _dot(tri_ref[...], hi) + (_dot(tri_ref[...], mid) + _dot(tri_ref[...], lo))

    cl = _each(cumsum, lw)
    last = [x[RW_CHUNK - 1:RW_CHUNK] for x in cl]
    e_neg = [jnp.exp(-x) for x in cl]
    e_end = _each(lambda l, x: jnp.exp(l - x), last, cl)

    lane_head = lax.broadcasted_iota(jnp.int32, (RW_CHUNK, W_R), 1) // D_R

    def stack(x):
        return jnp.concatenate([jnp.where(lane_head == h, x, 0.0) for h in range(H_R)], axis=0)

    m_a = _each(lambda x, c_, l_: stack(x * jnp.exp(c_ - l_)), av, cl, lw)
    m_b = _each(lambda x, e: stack(x * e), bv, e_neg)
    m_k = _each(lambda x, e: stack(x * e), k, e_neg)
    m_r = _each(lambda x, c_: stack(x * jnp.exp(c_)), r, cl)
    m_v = _each(stack, v)
    m_bh = _each(lambda x, e: stack(x * e), bv, e_end)
    m_kh = _each(lambda x, e: stack(x * e), k, e_end)

    row = lax.broadcasted_iota(jnp.int32, (RW_N, RW_N), 0)
    col = lax.broadcasted_iota(jnp.int32, (RW_N, RW_N), 1)
    mm = functools.partial(_mm, passes=passes)
    nt = functools.partial(mm, kind="nt")
    tn = functools.partial(mm, kind="tn")
    add = lambda x, y: x + y
    t_ab = _each(lambda x, y: jnp.where(row > col, nt(x, y), 0.0), m_a, m_b)
    t_ak = _each(lambda x, y: jnp.where(row > col, nt(x, y), 0.0), m_a, m_k)
    t_rb = _each(lambda x, y: jnp.where(row >= col, nt(x, y), 0.0), m_r, m_b)
    t_rk = _each(lambda x, y: jnp.where(row >= col, nt(x, y), 0.0), m_r, m_k)

    inv = [jnp.where(row == col, 1.0, t) for t in t_ab]
    pw = t_ab
    for _ in range(5):
        pw = _each(mm, pw, pw)
        inv = _each(add, inv, _each(mm, inv, pw))

    w1 = _each(mm, inv, m_a)
    w2 = _each(mm, inv, _each(mm, t_ak, m_v))
    rm = _each(add, m_r, _each(mm, t_rb, w1))
    yc = _each(add, _each(mm, t_rb, w2), _each(mm, t_rk, m_v))
    pm = _each(lambda l, x, y: jnp.where(row == col, jnp.exp(l), 0.0) + tn(x, y), last, w1, m_bh)
    qm = _each(add, _each(tn, w2, m_bh), _each(tn, m_v, m_kh))

    s = [s_ref[n] for n in seqs]
    y_st = _each(add, _each(nt, rm, s), yc)
    s_new = _each(lambda x, p_, q_: _mm(x, p_, passes=max(passes, 2)) + q_, s, pm, qm)
    for n in seqs:
        s_ref[n] = s_new[n]
    y = [(t[0:RW_CHUNK] + t[RW_CHUNK:2 * RW_CHUNK]) + (t[2 * RW_CHUNK:3 * RW_CHUNK] + t[3 * RW_CHUNK:])
         for t in y_st]

    mean = [_head_sum(x, ones) * (1.0 / D_R) for x in y]
    yc0 = _each(lambda x, m: x - m, y, mean)
    var = [_head_sum(x * x, ones) * (1.0 / D_R) for x in yc0]
    yn = _each(lambda x, vv: x * lax.rsqrt(vv + GN_EPS) * lnw_ref[...] + lnb_ref[...], yc0, var)
    bonus = _each(lambda rr, kk_, vv: _head_sum(rr * kk_ * rk_ref[...], ones) * vv, r, k, v)
    for n in seqs:
        o_ref[n] = (yn[n] + bonus[n]) * g[n]

    @pl.when(c == pl.num_programs(1) - 1)
    def _():
        sf_ref[...] = s_ref[...]


def _rwkv(cslab, shift0, s0bd, p, valid, passes=1):
    b, l, _ = cslab.shape
    row = lambda a: a.reshape(1, -1)
    pad_up = lambda w, off: jnp.zeros((LANES, W_R), F32).at[off:off + w.shape[0]].set(w).astype(BF16)
    t = np.arange(RW_CHUNK)[:, None]
    s = np.arange(RW_CHUNK)[None, :]
    consts = [row(p['rwkv_mu']), row(p['rwkv_w0']), pad_up(p['rwkv_w_up'], 0), row(p['rwkv_a0']),
              pad_up(p['rwkv_a_up'], 32), pad_up(p['rwkv_g_up'], 64), row(p['rwkv_k_k']),
              row(p['rwkv_k_a']), row(p['rwkv_r_k']), row(p['rwkv_ln_w']), row(p['rwkv_ln_b']),
              jnp.asarray((s <= t).astype(np.float32), dtype=BF16), _head_ones(W_R)]
    assert b % RW_SEQS == 0
    st = pl.BlockSpec((RW_SEQS, RW_N, RW_N), lambda i, c: (i, 0, 0))
    return pl.pallas_call(
        functools.partial(_rwkv_kernel, valid=valid, passes=passes),
        grid=(b // RW_SEQS, l // RW_CHUNK),
        in_specs=[pl.BlockSpec((RW_SEQS, RW_CHUNK, C_PROJ), lambda i, c: (i, c, 0)),
                  pl.BlockSpec((RW_SEQS, 1, C_PROJ), lambda i, c: (i, 0, 0)), st]
                 + [_const_spec(a.shape) for a in consts],
        out_specs=[pl.BlockSpec((RW_SEQS, RW_CHUNK, W_R), lambda i, c: (i, c, 0)), st],
        out_shape=[jax.ShapeDtypeStruct((b, l, W_R), F32),
                   jax.ShapeDtypeStruct((b, RW_N, RW_N), F32)],
        scratch_shapes=[pltpu.VMEM((RW_SEQS, 1, C_PROJ), F32), pltpu.VMEM((RW_SEQS, RW_N, RW_N), F32)],
        compiler_params=_cparams(("parallel", "arbitrary")),
        name="rwkv",
    )(cslab, shift0, s0bd, *consts)


def _quad_state_in(s):
    b = s.shape[0]
    return jnp.einsum('bhij,hg->bhigj', s, jnp.eye(H_R, dtype=s.dtype)).reshape(b, RW_N, RW_N)


def _quad_state_out(sbd):
    b = sbd.shape[0]
    s5 = sbd.reshape(b, H_R, D_R, H_R, D_R)
    return jnp.stack([s5[:, h, :, h, :] for h in range(H_R)], axis=1)


def _pad_tokens(a, multiple):
    pad = (-a.shape[1]) % multiple
    return a if pad == 0 else jnp.pad(a, ((0, 0), (0, pad), (0, 0)))


def _trunk_layer(x, b, l, sb_fn, s_hgrn0, s_rwkv0, shift0, mem_kv, lb, p, final_gain, prompt=False,
                 valid=None):
    valid = l if valid is None else valid
    if prompt:
        q, k, v, bslab, cslab = _in_proj_t(x, p['norm_mix'], p['w_in'], b, l)
        o_a = sb_fn(q.reshape(b, l, W_A), k[None], v[None])
    else:
        q, k, v, bslab, cslab = _in_proj(x, p['norm_mix'], p['w_in'])
        real = lambda a: a.reshape(b, l, W_A)[:, :valid]
        o_a = _pad_tokens(sb_fn(real(q), real(k), real(v)), l)
        k, v = real(k).reshape(b, valid, H_A, D_A), real(v).reshape(b, valid, H_A, D_A)
    o_b, s_hgrn = _hgrn(_pad_tokens(bslab.reshape(b, l, 4 * W_R), HG_SUB), jnp.log(lb), jnp.log1p(-lb),
                        p['hgrn_norm'], _pair_state_in(s_hgrn0), valid)
    cslab = cslab.reshape(b, l, C_PROJ)
    o_c, s_rwkv = _rwkv(_pad_tokens(cslab, RW_CHUNK), shift0.reshape(b, 1, C_PROJ),
                        _quad_state_in(s_rwkv0), p, valid)
    x1, qm = _out_q(x, o_a.reshape(b * l, W_A), o_b[:, :l].reshape(b * l, W_R),
                    o_c[:, :l].reshape(b * l, W_R), p['w_out'], p['norm_mem'], p['mem_wq'])
    x2 = _mem_attn(x1.reshape(b, l, D_MODEL), qm.reshape(b, l, D_MODEL), *mem_kv, p['mem_wo'])
    x3 = _ffn(x2.reshape(b * l, D_MODEL), p['norm_ffn'], p['ffn_w_gate'], p['ffn_w_up'], p['ffn_w_down'],
              p['norm_ffn'] if final_gain is None else final_gain, final_gain is not None)
    return x3, k, v, _pair_state_out(s_hgrn), _quad_state_out(s_rwkv), cslab[:, valid - 1]


def kernel(x_prompt, x_sample, mem_prompt, cache_sb_k, cache_sb_v, page_table, state_hgrn, state_rwkv,
           state_rwkv_shift, cache_mem_k, cache_mem_v, norm_mix, w_in, sb_bias, hgrn_lb, hgrn_norm, rwkv_mu,
           rwkv_w0, rwkv_w_up, rwkv_a0, rwkv_a_up, rwkv_g_up, rwkv_k_k, rwkv_k_a, rwkv_r_k, rwkv_ln_w, rwkv_ln_b,
           w_out, norm_mem, mem_wq, mem_wk, mem_wv, mem_wo, norm_ffn, ffn_w_gate, ffn_w_up, ffn_w_down,
           final_norm):
    bp, lp, _ = x_prompt.shape
    bs, ls, _ = x_sample.shape
    depth = w_in.shape[0]
    n_pool = cache_sb_k.shape[1]
    lb_all = jnp.cumsum(jax.nn.softmax(hgrn_lb.astype(F32), axis=0), axis=0)
    lb_all = lb_all - lb_all[0]
    assert page_table.shape[1] % SB_PAGES == 0 and x_sample.shape[1] == N_DEC
    cache_kt = jnp.transpose(cache_sb_k, (0, 1, 3, 4, 2))
    cache_vt = jnp.transpose(cache_sb_v, (0, 1, 3, 4, 2))
    mem_tok = mem_prompt.reshape(bp * N_MEM, D_MODEL)
    mem_k_s = cache_mem_k.reshape(depth, bs, N_MEM, D_MODEL)
    mem_v_s = cache_mem_v.reshape(depth, bs, N_MEM, D_MODEL)
    xp = x_prompt.reshape(bp * lp, D_MODEL)
    ls_pad = ls + (-ls) % SUBLANES
    xs = _pad_tokens(x_sample, SUBLANES).reshape(bs * ls_pad, D_MODEL)
    outs = [[] for _ in range(12)]
    for l in range(depth):
        bf = lambda w: w[l].astype(BF16)
        p = dict(norm_mix=norm_mix[l], w_in=bf(w_in), hgrn_norm=hgrn_norm[l], rwkv_mu=rwkv_mu[l],
                 rwkv_w0=rwkv_w0[l], rwkv_w_up=rwkv_w_up[l], rwkv_a0=rwkv_a0[l], rwkv_a_up=rwkv_a_up[l],
                 rwkv_g_up=rwkv_g_up[l], rwkv_k_k=rwkv_k_k[l], rwkv_k_a=rwkv_k_a[l], rwkv_r_k=rwkv_r_k[l],
                 rwkv_ln_w=rwkv_ln_w[l], rwkv_ln_b=rwkv_ln_b[l], w_out=bf(w_out), norm_mem=norm_mem[l],
                 mem_wq=bf(mem_wq), mem_wo=bf(mem_wo), norm_ffn=norm_ffn[l], ffn_w_gate=bf(ffn_w_gate),
                 ffn_w_up=bf(ffn_w_up), ffn_w_down=bf(ffn_w_down))
        gain = final_norm if l == depth - 1 else None
        bias = sb_bias[l]
        mk_p = _matmul(mem_tok, bf(mem_wk)).reshape(bp, N_MEM, D_MODEL)
        mv_p = _matmul(mem_tok, bf(mem_wv)).reshape(bp, N_MEM, D_MODEL)
        xp, kt_p, vt_p, hg_p, rw_p, sh_p = _trunk_layer(
            xp, bp, lp, functools.partial(_sb_prompt, bias=bias),
            jnp.zeros((bp, H_R, D_R, D_R), F32), jnp.zeros((bp, H_R, D_R, D_R), F32),
            jnp.zeros((bp, C_PROJ), F32), (mk_p[None], mv_p[None], 0), lb_all[l], p, gain, prompt=True)
        sb_fn = functools.partial(_sb_sample, bias=bias, cache_kt=cache_kt, cache_vt=cache_vt, layer=l,
                                  page_table=page_table)
        xs, ks, vs, hg_s, rw_s, sh_s = _trunk_layer(
            xs, bs, ls_pad, sb_fn, state_hgrn[l], state_rwkv[l], state_rwkv_shift[l],
            (mem_k_s, mem_v_s, l), lb_all[l], p, gain, valid=ls)
        heads_last = lambda a: jnp.transpose(a.reshape(bp, H_A, D_A, lp), (0, 3, 1, 2))
        for lst, val in zip(outs, (heads_last(kt_p), heads_last(vt_p), ks, vs, hg_p, hg_s, rw_p, rw_s, sh_p, sh_s,
                                   mk_p.reshape(bp, N_MEM, H_M, D_M), mv_p.reshape(bp, N_MEM, H_M, D_M))):
            lst.append(val)
    return ((xp.reshape(bp, lp, D_MODEL), xs.reshape(bs, ls_pad, D_MODEL)[:, :ls])
            + tuple(jnp.stack(o) for o in outs))
```
